```python
import math
import jax, jax.numpy as jnp
from jax import lax
import numpy as np

D_MODEL = 1024
BATCH = 16
SEQ = 2048
DEPTH = 4

D_CONV = D_MODEL
CONV_K = 3
D_SSM = D_MODEL
SSM_HEAD_DIM = 64
SSM_HEADS = D_SSM // SSM_HEAD_DIM
SSM_GROUPS = 2
HEADS_PER_GROUP = SSM_HEADS // SSM_GROUPS
D_STATE = 128
SSM_CONV_K = 4
CHUNK = 128
SSM_CONV_DIM = D_SSM + 2 * SSM_GROUPS * D_STATE
D_MIX = D_CONV + D_SSM
IN_COLS = 3 * D_CONV + D_SSM + SSM_CONV_DIM + SSM_HEADS
D_FF = 4 * D_MODEL
EPS = 1e-6

kernel_name = "hybrid_shortconv_ssd_parallel_groups"


def _rmsnorm(x, g):
    xf = x.astype(jnp.float32)
    y = xf * lax.rsqrt(jnp.mean(xf * xf, axis=-1, keepdims=True) + EPS)
    return (y * g.astype(jnp.float32)).astype(x.dtype)


def _causal_depthwise_conv(u, w):
    k, ch = w.shape
    return lax.conv_general_dilated(
        u, w[:, None, :].astype(u.dtype), window_strides=(1,), padding=[(k - 1, 0)],
        dimension_numbers=("NWC", "WIO", "NWC"), feature_group_count=ch)


def _ssd_chunked(x, dt, a, b, c):
    bsz, t = x.shape[:2]
    nc = t // CHUNK
    g, e, p, n = SSM_GROUPS, HEADS_PER_GROUP, SSM_HEAD_DIM, D_STATE
    xdt = (x * dt[..., None]).reshape(bsz, nc, CHUNK, g, e, p)
    adt = (dt * a).reshape(bsz, nc, CHUNK, g, e).transpose(0, 1, 3, 4, 2)
    bc = b.reshape(bsz, nc, CHUNK, g, n)
    cc = c.reshape(bsz, nc, CHUNK, g, n)
    cs = jnp.cumsum(adt, axis=-1)
    mask = jnp.tril(jnp.ones((CHUNK, CHUNK), dtype=bool))
    seg = jnp.where(mask, cs[..., :, None] - cs[..., None, :], -jnp.inf)
    decay_ls = jnp.exp(seg)
    scores = jnp.einsum("bclgn,bcsgn->bcgls", cc, bc)
    m = scores[:, :, :, None] * decay_ls
    y_diag = jnp.einsum("bcgels,bcsgep->bclgep", m, xdt)
    decay_to_end = jnp.exp(cs[..., -1:] - cs)
    states = jnp.einsum("bclgn,bcgel,bclgep->bcgepn", bc, decay_to_end, xdt)
    chunk_decay = jnp.exp(cs[..., -1])

    def step(carry, inp):
        s_c, d_c = inp
        return carry * d_c[..., None, None] + s_c, carry

    init = jnp.zeros((bsz, g, e, p, n), jnp.float32)
    _, prev = lax.scan(step, init, (jnp.moveaxis(states, 1, 0), jnp.moveaxis(chunk_decay, 1, 0)))
    prev = jnp.moveaxis(prev, 0, 1)
    y_off = jnp.einsum("bclgn,bcgepn,bcgel->bclgep", cc, prev, jnp.exp(cs))
    return (y_diag + y_off).reshape(bsz, t, SSM_HEADS, p)


def setup_inputs(seed: int = 0) -> dict:
    key = jax.random.key(seed)
    ks = jax.random.split(key, 20)
    f32 = jnp.float32
    nrm = lambda k, s, scale: jax.random.normal(k, s, f32) * scale
    gain = lambda k, s: 1.0 + 0.02 * jax.random.normal(k, s, f32)
    x = jax.random.normal(ks[0], (BATCH, SEQ, D_MODEL), f32)
    dt_min, dt_max = 1e-3, 1e-1
    u = jax.random.uniform(ks[6], (DEPTH, SSM_HEADS), f32)
    dt0 = jnp.exp(u * (math.log(dt_max) - math.log(dt_min)) + math.log(dt_min))
    dt_bias = dt0 + jnp.log(-jnp.expm1(-dt0))
    a_log = jnp.log(jax.random.uniform(ks[7], (DEPTH, SSM_HEADS), f32, 1.0, 16.0))
    return {
        "x": x,
        "norm_mix_pre": gain(ks[1], (DEPTH, D_MODEL)),
        "w_in": nrm(ks[2], (DEPTH, D_MODEL, IN_COLS), D_MODEL ** -0.5),
        "conv_a_w": nrm(ks[3], (DEPTH, CONV_K, D_CONV), CONV_K ** -0.5),
        "ssm_conv_w": nrm(ks[4], (DEPTH, SSM_CONV_K, SSM_CONV_DIM), SSM_CONV_K ** -0.5),
        "ssm_conv_b": nrm(ks[5], (DEPTH, SSM_CONV_DIM), 0.02),
        "dt_bias": dt_bias,
        "a_log": a_log,
        "d_skip": gain(ks[8], (DEPTH, SSM_HEADS)),
        "conv_out_norm": gain(ks[9], (DEPTH, D_CONV)),
        "ssm_out_norm": gain(ks[10], (DEPTH, D_SSM)),
        "w_out": nrm(ks[11], (DEPTH, D_MIX, D_MODEL), D_MIX ** -0.5),
        "norm_mix_post": gain(ks[12], (DEPTH, D_MODEL)),
        "norm_mlp_pre": gain(ks[13], (DEPTH, D_MODEL)),
        "w_up": nrm(ks[14], (DEPTH, D_MODEL, D_FF), D_MODEL ** -0.5),
        "w_down": nrm(ks[15], (DEPTH, D_FF, D_MODEL), D_FF ** -0.5),
        "norm_mlp_post": gain(ks[16], (DEPTH, D_MODEL)),
    }


def reference(x, norm_mix_pre, w_in, conv_a_w, ssm_conv_w, ssm_conv_b, dt_bias, a_log, d_skip,
              conv_out_norm, ssm_out_norm, w_out, norm_mix_post, norm_mlp_pre, w_up, w_down,
              norm_mlp_post):
    bsz, t, _ = x.shape
    split_at = [D_CONV, 2 * D_CONV, 3 * D_CONV, 3 * D_CONV + D_SSM,
                3 * D_CONV + D_SSM + SSM_CONV_DIM]
    for i in range(DEPTH):
        h = _rmsnorm(x, norm_mix_pre[i])
        proj = jnp.einsum("btd,de->bte", h, w_in[i])
        x_a, c_a, b_a, z, xbc, dt_raw = jnp.split(proj, split_at, axis=-1)
        y_a = b_a * _causal_depthwise_conv(c_a * x_a, conv_a_w[i])
        y_a = _rmsnorm(y_a, conv_out_norm[i])
        xbc = _causal_depthwise_conv(xbc, ssm_conv_w[i]) + ssm_conv_b[i].astype(xbc.dtype)
        xbc = jax.nn.silu(xbc)
        xs, bs, cs_ = jnp.split(xbc, [D_SSM, D_SSM + SSM_GROUPS * D_STATE], axis=-1)
        xs = xs.reshape(bsz, t, SSM_HEADS, SSM_HEAD_DIM).astype(jnp.float32)
        bs = bs.reshape(bsz, t, SSM_GROUPS, D_STATE).astype(jnp.float32)
        cs_ = cs_.reshape(bsz, t, SSM_GROUPS, D_STATE).astype(jnp.float32)
        dt = jax.nn.softplus(dt_raw.astype(jnp.float32) + dt_bias[i].astype(jnp.float32))
        a = -jnp.exp(a_log[i].astype(jnp.float32))
        y_s = _ssd_chunked(xs, dt, a, bs, cs_) + d_skip[i].astype(jnp.float32)[:, None] * xs
        y_s = y_s.reshape(bsz, t, D_SSM) * jax.nn.silu(z.astype(jnp.float32))
        y_s = _rmsnorm(y_s.reshape(bsz, t, SSM_GROUPS, D_SSM // SSM_GROUPS),
                       ssm_out_norm[i].reshape(SSM_GROUPS, D_SSM // SSM_GROUPS))
        y_s = y_s.reshape(bsz, t, D_SSM).astype(x.dtype)
        mix = jnp.einsum("bte,ed->btd", jnp.concatenate([y_a, y_s], axis=-1), w_out[i])
        x = x + _rmsnorm(mix, norm_mix_post[i])
        h = _rmsnorm(x, norm_mlp_pre[i])
        f = jnp.square(jax.nn.relu(jnp.einsum("btd,df->btf", h, w_up[i])))
        f = jnp.einsum("btf,fd->btd", f, w_down[i])
        x = x + _rmsnorm(f, norm_mlp_post[i])
    return x
```

```python
import functools

import jax
import jax.numpy as jnp
from jax import lax
from jax.experimental import pallas as pl
from jax.experimental.pallas import tpu as pltpu

F32 = jnp.float32
BF16 = jnp.bfloat16

D_MODEL = 1024
D_CONV = 1024
CONV_K = 3
D_SSM = 1024
HEAD_DIM = 64
N_HEADS = D_SSM // HEAD_DIM
N_GROUPS = 2
HEADS_PER_GROUP = N_HEADS // N_GROUPS
D_STATE = 128
SSM_CONV_K = 4
CHUNK = 128
XBC_DIM = D_SSM + 2 * N_GROUPS * D_STATE
GROUP_W = HEADS_PER_GROUP * HEAD_DIM
D_FF = 4 * D_MODEL
EPS = 1e-6

LANES = 128
SUBLANES = 8
DT_PAD = LANES
MAIN_COLS = 3 * D_CONV + D_SSM + XBC_DIM
IN_COLS_PAD = MAIN_COLS + DT_PAD
OFF_XA, OFF_CA, OFF_BA, OFF_Z = 0, D_CONV, 2 * D_CONV, 3 * D_CONV
OFF_XBC = 3 * D_CONV + D_SSM
OFF_DT = MAIN_COLS

MIX_TQ = 256
MLP_TM = 512
HALO = SUBLANES
VMEM_LIMIT = 56 * 1024 * 1024


def _rms(v, g):
    ms = jnp.mean(v * v, axis=-1, keepdims=True)
    return v * lax.rsqrt(ms + EPS) * g


def _split_bf16(v):
    hi = v.astype(BF16)
    lo = (v - hi.astype(F32)).astype(BF16)
    return hi, lo


def _mixer_kernel(x_ref, gpre_ref, win_ref, cwa_ref, cws_ref, cbs_ref, dtb_ref, alog_ref,
                  dsk_ref, gconv_ref, gssm_ref, wout_ref, gpost_ref, expand_ref,
                  o_ref, ubuf, xbuf, state, ycat):
    tq = x_ref.shape[1]

    @pl.when(pl.program_id(1) == 0)
    def _():
        ubuf[0:HALO, :] = jnp.zeros((HALO, D_CONV), F32)
        xbuf[0:HALO, :] = jnp.zeros((HALO, XBC_DIM), F32)
        state[...] = jnp.zeros(state.shape, F32)

    x = x_ref[0]
    h = _rms(x, gpre_ref[...]).astype(BF16)

    def proj(c0, width):
        return jnp.dot(h, win_ref[:, c0:c0 + width], preferred_element_type=F32)

    ubuf[HALO:HALO + tq, :] = proj(OFF_XA, D_CONV) * proj(OFF_CA, D_CONV)
    conv = cwa_ref[CONV_K - 1:CONV_K, :] * ubuf[HALO:HALO + tq, :]
    for k in range(CONV_K - 1):
        shift = CONV_K - 1 - k
        conv = conv + cwa_ref[k:k + 1, :] * ubuf[HALO - shift:HALO - shift + tq, :]
    ya = proj(OFF_BA, D_CONV) * conv
    ycat[:, 0:D_CONV] = _rms(ya, gconv_ref[...]).astype(BF16)
    ubuf[0:HALO, :] = ubuf[tq:tq + HALO, :]

    xbuf[HALO:HALO + tq, :] = proj(OFF_XBC, XBC_DIM)
    acc = cbs_ref[...] + cws_ref[SSM_CONV_K - 1:SSM_CONV_K, :] * xbuf[HALO:HALO + tq, :]
    for k in range(SSM_CONV_K - 1):
        shift = SSM_CONV_K - 1 - k
        acc = acc + cws_ref[k:k + 1, :] * xbuf[HALO - shift:HALO - shift + tq, :]
    xbc = acc * jax.nn.sigmoid(acc)
    xbuf[0:HALO, :] = xbuf[tq:tq + HALO, :]

    z = proj(OFF_Z, D_SSM)
    gate = z * jax.nn.sigmoid(z)
    dt = jax.nn.softplus(proj(OFF_DT, DT_PAD) + dtb_ref[...])
    a_row = -jnp.exp(alog_ref[...])

    rows = lax.broadcasted_iota(jnp.int32, (CHUNK, CHUNK), 0)
    cols = lax.broadcasted_iota(jnp.int32, (CHUNK, CHUNK), 1)
    causal = rows >= cols
    tri = causal.astype(BF16)
    tri2 = jnp.concatenate([tri, tri], axis=1)
    first_head_lanes = cols < HEAD_DIM

    for c in range(tq // CHUNK):
        r0 = c * CHUNK
        dt_c = dt[r0:r0 + CHUNK]
        adt = dt_c * a_row
        adt_hi, adt_lo = _split_bf16(adt)
        cs = jnp.dot(tri2, jnp.concatenate([adt_hi, adt_lo], axis=0),
                     preferred_element_type=F32)
        cs_last = cs[CHUNK - 1:CHUNK, :]
        stack = jnp.concatenate(
            [dt_c, cs_last - cs, jnp.broadcast_to(cs_last, (SUBLANES, DT_PAD))], axis=0)
        s_hi, s_lo = _split_bf16(stack)
        wide = jnp.dot(jnp.concatenate([s_hi, s_lo], axis=1), expand_ref[...],
                       preferred_element_type=F32)
        dt_w = wide[0:CHUNK]
        to_end_w = jnp.exp(wide[CHUNK:2 * CHUNK])
        chunk_decay_w = jnp.exp(wide[2 * CHUNK:2 * CHUNK + 1])

        xs_c = xbc[r0:r0 + CHUNK, 0:D_SSM]
        xdt = xs_c * dt_w
        cs_t = cs.T
        ecs = jnp.exp(cs)
        y_blocks = []
        for g in range(N_GROUPS):
            b_g = xbc[r0:r0 + CHUNK, D_SSM + g * D_STATE:D_SSM + (g + 1) * D_STATE]
            c_off = D_SSM + N_GROUPS * D_STATE
            c_g = xbc[r0:r0 + CHUNK, c_off + g * D_STATE:c_off + (g + 1) * D_STATE]
            scores = lax.dot_general(c_g.astype(BF16), b_g.astype(BF16),
                                     (((1,), (1,)), ((), ())), preferred_element_type=F32)
            s_prev = state[g]
            lane0 = g * GROUP_W
            for j in range(HEADS_PER_GROUP // 2):
                rhs = jnp.concatenate(
                    [xdt[:, lane0 + j * LANES:lane0 + (j + 1) * LANES],
                     s_prev[:, j * LANES:(j + 1) * LANES]], axis=0).astype(BF16)
                ys = []
                for e in range(2):
                    hh = g * HEADS_PER_GROUP + 2 * j + e
                    seg = jnp.where(causal, cs[:, hh:hh + 1] - cs_t[hh:hh + 1, :], -jnp.inf)
                    m = scores * jnp.exp(seg)
                    c_scaled = c_g * ecs[:, hh:hh + 1]
                    lhs = jnp.concatenate([m, c_scaled], axis=1).astype(BF16)
                    ys.append(jnp.dot(lhs, rhs, preferred_element_type=F32))
                y_blocks.append(jnp.where(first_head_lanes, ys[0], ys[1]))
            xdt_end = (xdt[:, lane0:lane0 + GROUP_W]
                       * to_end_w[:, lane0:lane0 + GROUP_W]).astype(BF16)
            contrib = jnp.dot(b_g.T.astype(BF16), xdt_end, preferred_element_type=F32)
            state[g] = s_prev * chunk_decay_w[:, lane0:lane0 + GROUP_W] + contrib
        y_c = jnp.concatenate(y_blocks, axis=1) + dsk_ref[...] * xs_c
        y_c = y_c * gate[r0:r0 + CHUNK]
        gs = gssm_ref[...]
        normed = [
            _rms(y_c[:, g * GROUP_W:(g + 1) * GROUP_W], gs[:, g * GROUP_W:(g + 1) * GROUP_W])
            for g in range(N_GROUPS)]
        ycat[r0:r0 + CHUNK, D_CONV:D_CONV + D_SSM] = jnp.concatenate(normed, axis=1).astype(BF16)

    mix = jnp.dot(ycat[...], wout_ref[...], preferred_element_type=F32)
    o_ref[0] = x + _rms(mix, gpost_ref[...])


def _mlp_kernel(x_ref, gpre_ref, wup_ref, wdown_ref, gpost_ref, o_ref):
    x = x_ref[...]
    h = _rms(x, gpre_ref[...]).astype(BF16)
    f = jnp.dot(h, wup_ref[...], preferred_element_type=F32)
    f = jnp.square(jnp.maximum(f, 0.0)).astype(BF16)
    f = jnp.dot(f, wdown_ref[...], preferred_element_type=F32)
    o_ref[...] = x + _rms(f, gpost_ref[...])


def _resident(shape):
    nd = len(shape)
    return pl.BlockSpec(shape, lambda *_: (0,) * nd, pipeline_mode=pl.Buffered(1))


def _mixer_call(x, params):
    bsz, t, d = x.shape
    tq = MIX_TQ
    x_spec = pl.BlockSpec((1, tq, d), lambda b, s: (b, s, 0))
    return pl.pallas_call(
        _mixer_kernel,
        grid=(bsz, t // tq),
        in_specs=[x_spec] + [_resident(p.shape) for p in params],
        out_specs=x_spec,
        out_shape=jax.ShapeDtypeStruct(x.shape, x.dtype),
        scratch_shapes=[
            pltpu.VMEM((tq + HALO, D_CONV), F32),
            pltpu.VMEM((tq + HALO, XBC_DIM), F32),
            pltpu.VMEM((N_GROUPS, D_STATE, GROUP_W), F32),
            pltpu.VMEM((tq, D_CONV + D_SSM), BF16),
        ],
        compiler_params=pltpu.CompilerParams(
            dimension_semantics=("arbitrary", "arbitrary"), vmem_limit_bytes=VMEM_LIMIT),
        name="mixer",
    )(x, *params)


def _mlp_call(x2, params):
    m, d = x2.shape
    x_spec = pl.BlockSpec((MLP_TM, d), lambda i: (i, 0))
    return pl.pallas_call(
        _mlp_kernel,
        grid=(m // MLP_TM,),
        in_specs=[x_spec] + [_resident(p.shape) for p in params],
        out_specs=x_spec,
        out_shape=jax.ShapeDtypeStruct(x2.shape, x2.dtype),
        compiler_params=pltpu.CompilerParams(
            dimension_semantics=("arbitrary",), vmem_limit_bytes=VMEM_LIMIT),
        name="mlp",
    )(x2, *params)


def _expand_matrix():
    k = lax.broadcasted_iota(jnp.int32, (2 * DT_PAD, D_SSM), 0) % DT_PAD
    j = lax.broadcasted_iota(jnp.int32, (2 * DT_PAD, D_SSM), 1) // HEAD_DIM
    return (k == j).astype(BF16)


def kernel(x, norm_mix_pre, w_in, conv_a_w, ssm_conv_w, ssm_conv_b, dt_bias, a_log, d_skip,
           conv_out_norm, ssm_out_norm, w_out, norm_mix_post, norm_mlp_pre, w_up, w_down,
           norm_mlp_post):
    bsz, t, d = x.shape
    depth = w_in.shape[0]
    assert d == D_MODEL and t % MIX_TQ == 0 and (bsz * t) % MLP_TM == 0
    assert w_in.shape[2] == MAIN_COLS + N_HEADS
    expand = _expand_matrix()
    row = lambda v: v.reshape(1, -1).astype(F32)
    pad_heads = lambda v: jnp.pad(v.astype(F32), (0, DT_PAD - N_HEADS)).reshape(1, DT_PAD)
    for i in range(depth):
        w_in_p = jnp.pad(w_in[i], ((0, 0), (0, DT_PAD - N_HEADS))).astype(BF16)
        mixer_params = (
            row(norm_mix_pre[i]), w_in_p, conv_a_w[i].astype(F32), ssm_conv_w[i].astype(F32),
            row(ssm_conv_b[i]), pad_heads(dt_bias[i]), pad_heads(a_log[i]),
            row(jnp.repeat(d_skip[i], HEAD_DIM)), row(conv_out_norm[i]), row(ssm_out_norm[i]),
            w_out[i].astype(BF16), row(norm_mix_post[i]), expand)
        x = _mixer_call(x, mixer_params)
        mlp_params = (row(norm_mlp_pre[i]), w_up[i].astype(BF16), w_down[i].astype(BF16),
                      row(norm_mlp_post[i]))
        x = _mlp_call(x.reshape(bsz * t, d), mlp_params).reshape(bsz, t, d)
    return x
```

```python
import functools

import jax
import jax.numpy as jnp
from jax import lax
from jax.experimental import pallas as pl
from jax.experimental.pallas import tpu as pltpu

F32 = jnp.float32
BF16 = jnp.bfloat16

D_MODEL = 1024
D_CONV = 1024
CONV_K = 3
D_SSM = 1024
HEAD_DIM = 64
N_HEADS = D_SSM // HEAD_DIM
N_GROUPS = 2
HEADS_PER_GROUP = N_HEADS // N_GROUPS
D_STATE = 128
SSM_CONV_K = 4
CHUNK = 128
XBC_DIM = D_SSM + 2 * N_GROUPS * D_STATE
GROUP_W = HEADS_PER_GROUP * HEAD_DIM
D_FF = 4 * D_MODEL
EPS = 1e-6

LANES = 128
SUBLANES = 8
DT_PAD = LANES
MAIN_COLS = 3 * D_CONV + D_SSM + XBC_DIM
IN_COLS_PAD = MAIN_COLS + DT_PAD
OFF_XA, OFF_CA, OFF_BA, OFF_Z = 0, D_CONV, 2 * D_CONV, 3 * D_CONV
OFF_XBC = 3 * D_CONV + D_SSM
OFF_DT = MAIN_COLS

MIX_TQ = 256
MLP_TM = 512
PROJ_W = 256
ROW_BLK = 64
CONV_B_W = 512
HALO = SUBLANES
VMEM_LIMIT = 56 * 1024 * 1024


def _rms(v, g):
    ms = jnp.mean(v * v, axis=-1, keepdims=True)
    return v * lax.rsqrt(ms + EPS) * g


def _split_bf16(v):
    hi = v.astype(BF16)
    lo = (v - hi.astype(F32)).astype(BF16)
    return hi, lo


def _project_steps(x, gpre_ref, win_ref, dst):
    h = _rms(x, gpre_ref[...]).astype(BF16)
    for c0 in range(0, IN_COLS_PAD, PROJ_W):
        width = min(PROJ_W, IN_COLS_PAD - c0)
        dst[:, c0:c0 + width] = jnp.dot(h, win_ref[:, c0:c0 + width],
                                        preferred_element_type=F32)
        yield


N_PROJ_STEPS = -(-IN_COLS_PAD // PROJ_W)
W_COPY, W_CONV_A, W_CONV_B, W_CHUNK_PRE, W_HEAD_PAIR, W_STATE, W_CHUNK_POST = (
    1.0, 3.0, 1.4, 3.0, 1.5, 1.0, 3.0)


def _mix_total_weight(tq):
    n_rb, n_cb, n_ch = tq // ROW_BLK, XBC_DIM // CONV_B_W, tq // CHUNK
    per_chunk = W_CHUNK_PRE + N_GROUPS * (HEADS_PER_GROUP // 2 * W_HEAD_PAIR + W_STATE) + W_CHUNK_POST
    return W_COPY + n_rb * W_CONV_A + n_rb * n_cb * W_CONV_B + n_ch * per_chunk


def _mix_steps(x, out, src, cwa_ref, cws_ref, cbs_ref, dtb_ref, alog_ref, dsk_ref, gconv_ref,
               gssm_ref, wout_ref, gpost_ref, expand_ref, ubuf, xbuf, act, state, ycat):
    tq = x.shape[0]

    yield W_COPY
    ubuf[HALO:HALO + tq, :] = src[:, OFF_XA:OFF_XA + D_CONV] * src[:, OFF_CA:OFF_CA + D_CONV]
    xbuf[HALO:HALO + tq, :] = src[:, OFF_XBC:OFF_XBC + XBC_DIM]

    for r0 in range(0, tq, ROW_BLK):
        yield W_CONV_A
        conv = cwa_ref[CONV_K - 1:CONV_K, :] * ubuf[HALO + r0:HALO + r0 + ROW_BLK, :]
        for k in range(CONV_K - 1):
            lo = HALO + r0 - (CONV_K - 1 - k)
            conv = conv + cwa_ref[k:k + 1, :] * ubuf[lo:lo + ROW_BLK, :]
        ya = src[r0:r0 + ROW_BLK, OFF_BA:OFF_BA + D_CONV] * conv
        ycat[r0:r0 + ROW_BLK, 0:D_CONV] = _rms(ya, gconv_ref[...]).astype(BF16)

    for r0 in range(0, tq, ROW_BLK):
        for c0 in range(0, XBC_DIM, CONV_B_W):
            yield W_CONV_B
            blk = slice(c0, c0 + CONV_B_W)
            acc = cbs_ref[:, blk] + (cws_ref[SSM_CONV_K - 1:SSM_CONV_K, blk]
                                     * xbuf[HALO + r0:HALO + r0 + ROW_BLK, blk])
            for k in range(SSM_CONV_K - 1):
                lo = HALO + r0 - (SSM_CONV_K - 1 - k)
                acc = acc + cws_ref[k:k + 1, blk] * xbuf[lo:lo + ROW_BLK, blk]
            act[r0:r0 + ROW_BLK, blk] = acc * jax.nn.sigmoid(acc)
    ubuf[0:HALO, :] = ubuf[tq:tq + HALO, :]
    xbuf[0:HALO, :] = xbuf[tq:tq + HALO, :]

    a_row = -jnp.exp(alog_ref[...])
    r_i = lax.broadcasted_iota(jnp.int32, (CHUNK, CHUNK), 0)
    c_i = lax.broadcasted_iota(jnp.int32, (CHUNK, CHUNK), 1)
    causal = r_i >= c_i
    tri = causal.astype(BF16)
    tri2 = jnp.concatenate([tri, tri], axis=1)
    first_head_lanes = c_i < HEAD_DIM
    c_off = D_SSM + N_GROUPS * D_STATE

    for c in range(tq // CHUNK):
        yield W_CHUNK_PRE
        rs = slice(c * CHUNK, (c + 1) * CHUNK)
        dt_c = jax.nn.softplus(src[rs, OFF_DT:OFF_DT + DT_PAD] + dtb_ref[...])
        adt = dt_c * a_row
        adt_hi, adt_lo = _split_bf16(adt)
        cs = jnp.dot(tri2, jnp.concatenate([adt_hi, adt_lo], axis=0),
                     preferred_element_type=F32)
        cs_last = cs[CHUNK - 1:CHUNK, :]
        stack = jnp.concatenate(
            [dt_c, cs_last - cs, jnp.broadcast_to(cs_last, (SUBLANES, DT_PAD))], axis=0)
        s_hi, s_lo = _split_bf16(stack)
        wide = jnp.dot(jnp.concatenate([s_hi, s_lo], axis=1), expand_ref[...],
                       preferred_element_type=F32)
        dt_w = wide[0:CHUNK]
        to_end_w = jnp.exp(wide[CHUNK:2 * CHUNK])
        chunk_decay_w = jnp.exp(wide[2 * CHUNK:2 * CHUNK + 1])

        xs_c = act[rs, 0:D_SSM]
        xdt = xs_c * dt_w
        cs_t = cs.T
        ecs = jnp.exp(cs)
        y_blocks = []
        for g in range(N_GROUPS):
            b_g = act[rs, D_SSM + g * D_STATE:D_SSM + (g + 1) * D_STATE]
            c_g = act[rs, c_off + g * D_STATE:c_off + (g + 1) * D_STATE]
            scores = lax.dot_general(c_g.astype(BF16), b_g.astype(BF16),
                                     (((1,), (1,)), ((), ())), preferred_element_type=F32)
            s_prev = state[g]
            lane0 = g * GROUP_W
            for j in range(HEADS_PER_GROUP // 2):
                yield W_HEAD_PAIR
                rhs = jnp.concatenate(
                    [xdt[:, lane0 + j * LANES:lane0 + (j + 1) * LANES],
                     s_prev[:, j * LANES:(j + 1) * LANES]], axis=0).astype(BF16)
                ys = []
                for e in range(2):
                    hh = g * HEADS_PER_GROUP + 2 * j + e
                    seg = jnp.where(causal, cs[:, hh:hh + 1] - cs_t[hh:hh + 1, :], -jnp.inf)
                    m = scores * jnp.exp(seg)
                    c_scaled = c_g * ecs[:, hh:hh + 1]
                    lhs = jnp.concatenate([m, c_scaled], axis=1).astype(BF16)
                    ys.append(jnp.dot(lhs, rhs, preferred_element_type=F32))
                y_blocks.append(jnp.where(first_head_lanes, ys[0], ys[1]))
            yield W_STATE
            xdt_end = (xdt[:, lane0:lane0 + GROUP_W]
                       * to_end_w[:, lane0:lane0 + GROUP_W]).astype(BF16)
            contrib = jnp.dot(b_g.T.astype(BF16), xdt_end, preferred_element_type=F32)
            state[g] = s_prev * chunk_decay_w[:, lane0:lane0 + GROUP_W] + contrib
        yield W_CHUNK_POST
        z = src[rs, OFF_Z:OFF_Z + D_SSM]
        y_c = jnp.concatenate(y_blocks, axis=1) + dsk_ref[...] * xs_c
        y_c = y_c * (z * jax.nn.sigmoid(z))
        gs = gssm_ref[...]
        normed = [
            _rms(y_c[:, g * GROUP_W:(g + 1) * GROUP_W], gs[:, g * GROUP_W:(g + 1) * GROUP_W])
            for g in range(N_GROUPS)]
        ycat[rs, D_CONV:D_CONV + D_SSM] = jnp.concatenate(normed, axis=1).astype(BF16)

    yield 0.0
    mix = jnp.dot(ycat[...], wout_ref[...], preferred_element_type=F32)
    out[...] = x + _rms(mix, gpost_ref[...])


def _run_interleaved(main, total_weight, fillers, n_fillers):
    done, placed = 0.0, 0
    for weight in main:
        done += weight
        while placed < min(n_fillers, round(n_fillers * done / total_weight)):
            next(fillers)
            placed += 1
    for _ in fillers:
        pass


def _mixer_kernel(steps_per_seq, x_ref, xnext_ref, gpre_ref, win_ref, cwa_ref, cws_ref, cbs_ref,
                  dtb_ref, alog_ref, dsk_ref, gconv_ref, gssm_ref, wout_ref, gpost_ref,
                  expand_ref, o_ref, pbuf0, pbuf1, ubuf, xbuf, act, state, ycat):
    step = pl.program_id(0)
    tq = MIX_TQ
    rest = (cwa_ref, cws_ref, cbs_ref, dtb_ref, alog_ref, dsk_ref, gconv_ref, gssm_ref,
            wout_ref, gpost_ref, expand_ref, ubuf, xbuf, act, state, ycat)
    lo, hi = slice(0, tq), slice(tq, 2 * tq)

    @pl.when(step == 0)
    def _():
        for _ in _project_steps(x_ref[lo, :], gpre_ref, win_ref, pbuf0):
            pass

    @pl.when(step % steps_per_seq == 0)
    def _():
        ubuf[0:HALO, :] = jnp.zeros((HALO, D_CONV), F32)
        xbuf[0:HALO, :] = jnp.zeros((HALO, XBC_DIM), F32)
        state[...] = jnp.zeros(state.shape, F32)

    total = _mix_total_weight(tq)
    _run_interleaved(_mix_steps(x_ref[lo, :], o_ref.at[lo, :], pbuf0, *rest), total,
                     _project_steps(x_ref[hi, :], gpre_ref, win_ref, pbuf1), N_PROJ_STEPS)
    _run_interleaved(_mix_steps(x_ref[hi, :], o_ref.at[hi, :], pbuf1, *rest), total,
                     _project_steps(xnext_ref[...], gpre_ref, win_ref, pbuf0), N_PROJ_STEPS)


def _mlp_kernel(x_ref, gpre_ref, wup_ref, wdown_ref, gpost_ref, o_ref):
    x = x_ref[...]
    h = _rms(x, gpre_ref[...]).astype(BF16)
    f = jnp.dot(h, wup_ref[...], preferred_element_type=F32)
    f = jnp.square(jnp.maximum(f, 0.0)).astype(BF16)
    f = jnp.dot(f, wdown_ref[...], preferred_element_type=F32)
    o_ref[...] = x + _rms(f, gpost_ref[...])


class _Layer:
    def __init__(self, stacked, index):
        self.array, self.index = stacked, index


def _resident(p):
    if isinstance(p, _Layer):
        nd = p.array.ndim - 1
        return pl.BlockSpec((None,) + p.array.shape[1:], lambda *_: (p.index,) + (0,) * nd,
                            pipeline_mode=pl.Buffered(1))
    nd = p.ndim
    return pl.BlockSpec(p.shape, lambda *_: (0,) * nd, pipeline_mode=pl.Buffered(1))


def _operand(p):
    return p.array if isinstance(p, _Layer) else p


def _mixer_call(x2, seq_len, params):
    m, d = x2.shape
    tq = MIX_TQ
    n_steps = m // (2 * tq)
    n_tiles = m // tq
    x_spec = pl.BlockSpec((2 * tq, d), lambda i: (i, 0))
    xnext_spec = pl.BlockSpec((tq, d), lambda i: (jnp.minimum(2 * i + 2, n_tiles - 1), 0))
    return pl.pallas_call(
        functools.partial(_mixer_kernel, seq_len // (2 * tq)),
        grid=(n_steps,),
        in_specs=[x_spec, xnext_spec] + [_resident(p) for p in params],
        out_specs=x_spec,
        out_shape=jax.ShapeDtypeStruct(x2.shape, x2.dtype),
        scratch_shapes=[
            pltpu.VMEM((tq, IN_COLS_PAD), F32),
            pltpu.VMEM((tq, IN_COLS_PAD), F32),
            pltpu.VMEM((tq + HALO, D_CONV), F32),
            pltpu.VMEM((tq + HALO, XBC_DIM), F32),
            pltpu.VMEM((tq, XBC_DIM), F32),
            pltpu.VMEM((N_GROUPS, D_STATE, GROUP_W), F32),
            pltpu.VMEM((tq, D_CONV + D_SSM), BF16),
        ],
        compiler_params=pltpu.CompilerParams(
            dimension_semantics=("arbitrary",), vmem_limit_bytes=VMEM_LIMIT),
        name="mixer",
    )(x2, x2, *[_operand(p) for p in params])


def _mlp_call(x2, params):
    m, d = x2.shape
    x_spec = pl.BlockSpec((MLP_TM, d), lambda i: (i, 0))
    return pl.pallas_call(
        _mlp_kernel,
        grid=(m // MLP_TM,),
        in_specs=[x_spec] + [_resident(p) for p in params],
        out_specs=x_spec,
        out_shape=jax.ShapeDtypeStruct(x2.shape, x2.dtype),
        compiler_params=pltpu.CompilerParams(
            dimension_semantics=("arbitrary",), vmem_limit_bytes=VMEM_LIMIT),
        name="mlp",
    )(x2, *[_operand(p) for p in params])


def _expand_matrix():
    k = lax.broadcasted_iota(jnp.int32, (2 * DT_PAD, D_SSM), 0) % DT_PAD
    j = lax.broadcasted_iota(jnp.int32, (2 * DT_PAD, D_SSM), 1) // HEAD_DIM
    return (k == j).astype(BF16)


def kernel(x, norm_mix_pre, w_in, conv_a_w, ssm_conv_w, ssm_conv_b, dt_bias, a_log, d_skip,
           conv_out_norm, ssm_out_norm, w_out, norm_mix_post, norm_mlp_pre, w_up, w_down,
           norm_mlp_post):
    bsz, t, d = x.shape
    depth = w_in.shape[0]
    assert d == D_MODEL and t % (2 * MIX_TQ) == 0 and (bsz * t) % MLP_TM == 0
    assert w_in.shape[2] == MAIN_COLS + N_HEADS
    expand = _expand_matrix()
    x = x.reshape(bsz * t, d)
    row = lambda v: v.reshape(1, -1).astype(F32)
    pad_heads = lambda v: jnp.pad(v.astype(F32), (0, DT_PAD - N_HEADS)).reshape(1, DT_PAD)
    w_in_p = jnp.pad(w_in.astype(BF16), ((0, 0), (0, 0), (0, DT_PAD - N_HEADS)))
    w_out_b, w_up_b, w_down_b = w_out.astype(BF16), w_up.astype(BF16), w_down.astype(BF16)
    for i in range(depth):
        mixer_params = (
            row(norm_mix_pre[i]), _Layer(w_in_p, i), conv_a_w[i].astype(F32),
            ssm_conv_w[i].astype(F32), row(ssm_conv_b[i]), pad_heads(dt_bias[i]),
            pad_heads(a_log[i]), row(jnp.repeat(d_skip[i], HEAD_DIM)), row(conv_out_norm[i]),
            row(ssm_out_norm[i]), _Layer(w_out_b, i), row(norm_mix_post[i]), expand)
        x = _mixer_call(x, t, mixer_params)
        mlp_params = (row(norm_mlp_pre[i]), _Layer(w_up_b, i), _Layer(w_down_b, i),
                      row(norm_mlp_post[i]))
        x = _mlp_call(x, mlp_params)
    return x.reshape(bsz, t, d)
```

```python
import functools

import jax
import jax.numpy as jnp
from jax import lax
from jax.experimental import pallas as pl
from jax.experimental.pallas import tpu as pltpu

F32 = jnp.float32
BF16 = jnp.bfloat16

D_MODEL = 1024
D_CONV = 1024
CONV_K = 3
D_SSM = 1024
HEAD_DIM = 64
N_HEADS = D_SSM // HEAD_DIM
N_GROUPS = 2
HEADS_PER_GROUP = N_HEADS // N_GROUPS
D_STATE = 128
SSM_CONV_K = 4
CHUNK = 128
XBC_DIM = D_SSM + 2 * N_GROUPS * D_STATE
GROUP_W = HEADS_PER_GROUP * HEAD_DIM
D_FF = 4 * D_MODEL
EPS = 1e-6

LANES = 128
SUBLANES = 8
DT_PAD = LANES
MAIN_COLS = 3 * D_CONV + D_SSM + XBC_DIM
OFF_XA, OFF_CA, OFF_BA, OFF_Z = 0, D_CONV, 2 * D_CONV, 3 * D_CONV
OFF_XBC = 3 * D_CONV + D_SSM

MIX_TQ = 256
MLP_TM = 1024
MLP_FF_BLK = 1024
COL_BLK = 256
TIE_LEAD = 3
VMEM_LIMIT = 56 * 1024 * 1024


def _rms(v, g):
    ms = jnp.mean(v * v, axis=-1, keepdims=True)
    return v * lax.rsqrt(ms + EPS) * g


def _silu(v):
    hv = 0.5 * v
    return hv * jnp.tanh(hv) + hv


def _split_bf16(v):
    hi = v.astype(BF16)
    lo = (v - hi.astype(F32)).astype(BF16)
    return hi, lo


def _after(v, dep):
    bits = lax.bitcast_convert_type(dep[0:v.shape[0], 0:v.shape[1]], jnp.uint32)
    sixteen = jnp.uint32(16)
    zero = lax.shift_right_logical(lax.shift_right_logical(bits, sixteen), sixteen)
    return lax.bitcast_convert_type(lax.bitcast_convert_type(v, jnp.uint32) | zero, F32)


def _row_groups(v):
    return [v[SUBLANES * i:SUBLANES * (i + 1)] for i in range(v.shape[0] // SUBLANES)]


def _causal_conv(v, prev, taps, bias=None):
    k_taps = len(taps)
    groups = [prev] + _row_groups(v)
    sub = lax.broadcasted_iota(jnp.int32, prev.shape, 0)
    out = [taps[k_taps - 1] * g for g in groups[1:]]
    if bias is not None:
        out = [o + bias for o in out]
    for k in range(k_taps - 1):
        s = k_taps - 1 - k
        rolled = [pltpu.roll(g, s, 0) for g in groups]
        for i in range(len(out)):
            out[i] = out[i] + taps[k] * jnp.where(sub < s, rolled[i], rolled[i + 1])
    return out


def _mixer_kernel(steps_per_seq, x_ref, gpre_ref, wmain_ref, wdt_ref, cwa_ref, cws_ref, cbs_ref,
                  dtb_ref, alog_ref, dsk_ref, gconv_ref, gssm_ref, wout_ref, gpost_ref,
                  expand_ref, o_ref, utail, xtail, yabuf, act, gate, state, ycat):
    tq = x_ref.shape[0]

    @pl.when(pl.program_id(0) % steps_per_seq == 0)
    def _():
        utail[...] = jnp.zeros(utail.shape, F32)
        xtail[...] = jnp.zeros(xtail.shape, F32)
        state[...] = jnp.zeros(state.shape, F32)

    x = x_ref[...]
    h = _rms(x, gpre_ref[...]).astype(BF16)

    def proj(c0, width=COL_BLK):
        return jnp.dot(h, wmain_ref[:, c0:c0 + width], preferred_element_type=F32)

    for c0 in range(0, XBC_DIM, COL_BLK):
        blk = slice(c0, c0 + COL_BLK)
        raw = proj(OFF_XBC + c0)
        taps = [cws_ref[k:k + 1, blk] for k in range(SSM_CONV_K)]
        conv = _causal_conv(raw, xtail[:, blk], taps, cbs_ref[:, blk])
        xtail[:, blk] = raw[tq - SUBLANES:tq]
        act[:, blk] = jnp.concatenate([_silu(g) for g in conv], axis=0)

    dt_raw = jnp.dot(h, wdt_ref[...], preferred_element_type=F32)
    dt = jax.nn.softplus(dt_raw + dtb_ref[...])

    proj_a = {}

    def group_a_block(key, col):
        proj_a[key] = proj(col)
        return proj_a[key]

    deferred = [functools.partial(group_a_block, (kind, c0), off + c0)
                for c0 in range(0, D_CONV, COL_BLK)
                for kind, off in (("xa", OFF_XA), ("ca", OFF_CA), ("ba", OFF_BA))]
    issued = []
    for c0 in range(0, D_SSM, COL_BLK):
        gate[:, c0:c0 + COL_BLK] = _silu(proj(OFF_Z + c0))

    a_row = -jnp.exp(alog_ref[...])
    r_i = lax.broadcasted_iota(jnp.int32, (CHUNK, CHUNK), 0)
    c_i = lax.broadcasted_iota(jnp.int32, (CHUNK, CHUNK), 1)
    causal = r_i >= c_i
    tri = causal.astype(BF16)
    tri2 = jnp.concatenate([tri, tri], axis=1)
    first_head_lanes = c_i < HEAD_DIM
    c_off = D_SSM + N_GROUPS * D_STATE

    for c in range(tq // CHUNK):
        rs = slice(c * CHUNK, (c + 1) * CHUNK)
        dt_c = dt[rs]
        adt = dt_c * a_row
        adt_hi, adt_lo = _split_bf16(adt)
        cs = jnp.dot(tri2, jnp.concatenate([adt_hi, adt_lo], axis=0),
                     preferred_element_type=F32)
        cs_last = cs[CHUNK - 1:CHUNK, :]
        stack = jnp.concatenate(
            [dt_c, cs_last - cs, jnp.broadcast_to(cs_last, (SUBLANES, DT_PAD))], axis=0)
        s_hi, s_lo = _split_bf16(stack)
        wide = jnp.dot(jnp.concatenate([s_hi, s_lo], axis=1), expand_ref[...],
                       preferred_element_type=F32)
        dt_w = wide[0:CHUNK]
        to_end_w = jnp.exp(wide[CHUNK:2 * CHUNK])
        chunk_decay_w = jnp.exp(wide[2 * CHUNK:2 * CHUNK + 1])

        xs_c = act[rs, 0:D_SSM]
        xdt = xs_c * dt_w
        cs_t = cs.T
        ecs = jnp.exp(cs)
        y_blocks = []
        for g in range(N_GROUPS):
            b_g = act[rs, D_SSM + g * D_STATE:D_SSM + (g + 1) * D_STATE]
            c_g = act[rs, c_off + g * D_STATE:c_off + (g + 1) * D_STATE]
            scores = lax.dot_general(c_g.astype(BF16), b_g.astype(BF16),
                                     (((1,), (1,)), ((), ())), preferred_element_type=F32)
            s_prev = state[g]
            lane0 = g * GROUP_W
            for j in range(HEADS_PER_GROUP // 2):
                xd = xdt[:, lane0 + j * LANES:lane0 + (j + 1) * LANES]
                if deferred:
                    issued.append(deferred.pop(0)())
                if len(issued) > TIE_LEAD:
                    xd = jnp.concatenate(
                        [_after(xd[0:SUBLANES], issued.pop(0)), xd[SUBLANES:]], axis=0)
                rhs = jnp.concatenate(
                    [xd, s_prev[:, j * LANES:(j + 1) * LANES]], axis=0).astype(BF16)
                ys = []
                for e in range(2):
                    hh = g * HEADS_PER_GROUP + 2 * j + e
                    seg = jnp.where(causal, cs[:, hh:hh + 1] - cs_t[hh:hh + 1, :], -jnp.inf)
                    m = scores * jnp.exp(seg)
                    c_scaled = c_g * ecs[:, hh:hh + 1]
                    lhs = jnp.concatenate([m, c_scaled], axis=1).astype(BF16)
                    ys.append(jnp.dot(lhs, rhs, preferred_element_type=F32))
                y_blocks.append(jnp.where(first_head_lanes, ys[0], ys[1]))
            xdt_end = (xdt[:, lane0:lane0 + GROUP_W]
                       * to_end_w[:, lane0:lane0 + GROUP_W]).astype(BF16)
            contrib = jnp.dot(b_g.T.astype(BF16), xdt_end, preferred_element_type=F32)
            state[g] = s_prev * chunk_decay_w[:, lane0:lane0 + GROUP_W] + contrib
        y_c = jnp.concatenate(y_blocks, axis=1) + dsk_ref[...] * xs_c
        y_c = y_c * gate[rs, :]
        gs = gssm_ref[...]
        normed = [
            _rms(y_c[:, g * GROUP_W:(g + 1) * GROUP_W], gs[:, g * GROUP_W:(g + 1) * GROUP_W])
            for g in range(N_GROUPS)]
        ycat[rs, D_CONV:D_CONV + D_SSM] = jnp.concatenate(normed, axis=1).astype(BF16)

    ssq = jnp.zeros((tq, 1), F32)
    for block in deferred:
        block()
    for c0 in range(0, D_CONV, COL_BLK):
        blk = slice(c0, c0 + COL_BLK)
        u = proj_a["xa", c0] * proj_a["ca", c0]
        taps = [cwa_ref[k:k + 1, blk] for k in range(CONV_K)]
        conv = jnp.concatenate(_causal_conv(u, utail[:, blk], taps), axis=0)
        utail[:, blk] = u[tq - SUBLANES:tq]
        ya = proj_a["ba", c0] * conv
        ssq = ssq + jnp.sum(ya * ya, axis=-1, keepdims=True)
        yabuf[:, blk] = ya
    ya_scale = lax.rsqrt(ssq * (1.0 / D_CONV) + EPS)
    ycat[:, 0:D_CONV] = (yabuf[...] * ya_scale * gconv_ref[...]).astype(BF16)

    mix = (jnp.dot(ycat[:, D_CONV:], wout_ref[D_CONV:, :], preferred_element_type=F32)
           + jnp.dot(ycat[:, 0:D_CONV], wout_ref[0:D_CONV, :], preferred_element_type=F32))
    o_ref[...] = x + _rms(mix, gpost_ref[...])


def _mlp_kernel(x_ref, gpre_ref, wup_ref, wdown_ref, gpost_ref, o_ref):
    x = x_ref[...]
    h = _rms(x, gpre_ref[...]).astype(BF16)
    acc = None
    for f0 in range(0, D_FF, MLP_FF_BLK):
        f = jnp.dot(h, wup_ref[:, f0:f0 + MLP_FF_BLK], preferred_element_type=F32)
        f = jnp.square(jnp.maximum(f, 0.0)).astype(BF16)
        part = jnp.dot(f, wdown_ref[f0:f0 + MLP_FF_BLK, :], preferred_element_type=F32)
        acc = part if acc is None else acc + part
    o_ref[...] = x + _rms(acc, gpost_ref[...])


class _Layer:
    def __init__(self, stacked, index):
        self.array, self.index = stacked, index


def _resident(p):
    if isinstance(p, _Layer):
        nd = p.array.ndim - 1
        return pl.BlockSpec((None,) + p.array.shape[1:], lambda *_: (p.index,) + (0,) * nd,
                            pipeline_mode=pl.Buffered(1))
    nd = p.ndim
    return pl.BlockSpec(p.shape, lambda *_: (0,) * nd, pipeline_mode=pl.Buffered(1))


def _operand(p):
    return p.array if isinstance(p, _Layer) else p


def _mixer_call(x2, seq_len, params):
    m, d = x2.shape
    tq = MIX_TQ
    x_spec = pl.BlockSpec((tq, d), lambda i: (i, 0))
    return pl.pallas_call(
        functools.partial(_mixer_kernel, seq_len // tq),
        grid=(m // tq,),
        in_specs=[x_spec] + [_resident(p) for p in params],
        out_specs=x_spec,
        out_shape=jax.ShapeDtypeStruct(x2.shape, x2.dtype),
        scratch_shapes=[
            pltpu.VMEM((SUBLANES, D_CONV), F32),
            pltpu.VMEM((SUBLANES, XBC_DIM), F32),
            pltpu.VMEM((tq, D_CONV), F32),
            pltpu.VMEM((tq, XBC_DIM), F32),
            pltpu.VMEM((tq, D_SSM), F32),
            pltpu.VMEM((N_GROUPS, D_STATE, GROUP_W), F32),
            pltpu.VMEM((tq, D_CONV + D_SSM), BF16),
        ],
        compiler_params=pltpu.CompilerParams(
            dimension_semantics=("arbitrary",), vmem_limit_bytes=VMEM_LIMIT),
        name="mixer",
    )(x2, *[_operand(p) for p in params])


def _mlp_call(x2, params):
    m, d = x2.shape
    x_spec = pl.BlockSpec((MLP_TM, d), lambda i: (i, 0))
    return pl.pallas_call(
        _mlp_kernel,
        grid=(m // MLP_TM,),
        in_specs=[x_spec] + [_resident(p) for p in params],
        out_specs=x_spec,
        out_shape=jax.ShapeDtypeStruct(x2.shape, x2.dtype),
        compiler_params=pltpu.CompilerParams(
            dimension_semantics=("arbitrary",), vmem_limit_bytes=VMEM_LIMIT),
        name="mlp",
    )(x2, *[_operand(p) for p in params])


def _expand_matrix():
    k = lax.broadcasted_iota(jnp.int32, (2 * DT_PAD, D_SSM), 0) % DT_PAD
    j = lax.broadcasted_iota(jnp.int32, (2 * DT_PAD, D_SSM), 1) // HEAD_DIM
    return (k == j).astype(BF16)


def kernel(x, norm_mix_pre, w_in, conv_a_w, ssm_conv_w, ssm_conv_b, dt_bias, a_log, d_skip,
           conv_out_norm, ssm_out_norm, w_out, norm_mix_post, norm_mlp_pre, w_up, w_down,
           norm_mlp_post):
    bsz, t, d = x.shape
    depth = w_in.shape[0]
    assert d == D_MODEL and t % MIX_TQ == 0 and (bsz * t) % MLP_TM == 0
    assert w_in.shape[2] == MAIN_COLS + N_HEADS
    expand = _expand_matrix()
    x = x.reshape(bsz * t, d)
    row = lambda v: v.reshape(1, -1).astype(F32)
    pad_heads = lambda v: jnp.pad(v.astype(F32), (0, DT_PAD - N_HEADS)).reshape(1, DT_PAD)
    w_main = w_in[:, :, :MAIN_COLS].astype(BF16)
    w_dt = jnp.pad(w_in[:, :, MAIN_COLS:], ((0, 0), (0, 0), (0, DT_PAD - N_HEADS))).astype(BF16)
    w_out_b, w_up_b, w_down_b = w_out.astype(BF16), w_up.astype(BF16), w_down.astype(BF16)
    for i in range(depth):
        mixer_params = (
            row(norm_mix_pre[i]), _Layer(w_main, i), _Layer(w_dt, i), conv_a_w[i].astype(F32),
            ssm_conv_w[i].astype(F32), row(ssm_conv_b[i]), pad_heads(dt_bias[i]),
            pad_heads(a_log[i]), row(jnp.repeat(d_skip[i], HEAD_DIM)), row(conv_out_norm[i]),
            row(ssm_out_norm[i]), _Layer(w_out_b, i), row(norm_mix_post[i]), expand)
        x = _mixer_call(x, t, mixer_params)
        mlp_params = (row(norm_mlp_pre[i]), _Layer(w_up_b, i), _Layer(w_down_b, i),
                      row(norm_mlp_post[i]))
        x = _mlp_call(x, mlp_params)
    return x.reshape(bsz, t, d)
```

```python
import functools

import jax
import jax.numpy as jnp
from jax import lax
from jax.experimental import pallas as pl
from jax.experimental.pallas import tpu as pltpu

F32 = jnp.float32
BF16 = jnp.bfloat16

D_MODEL = 1024
D_CONV = 1024
CONV_K = 3
D_SSM = 1024
HEAD_DIM = 64
N_HEADS = D_SSM // HEAD_DIM
N_GROUPS = 2
HEADS_PER_GROUP = N_HEADS // N_GROUPS
D_STATE = 128
SSM_CONV_K = 4
CHUNK = 128
XBC_DIM = D_SSM + 2 * N_GROUPS * D_STATE
GROUP_W = HEADS_PER_GROUP * HEAD_DIM
D_FF = 4 * D_MODEL
EPS = 1e-6
LOG2E = 1.4426950408889634

LANES = 128
SUBLANES = 8
DT_PAD = LANES
MAIN_COLS = 3 * D_CONV + D_SSM + XBC_DIM
OFF_XA, OFF_CA, OFF_BA, OFF_Z = 0, D_CONV, 2 * D_CONV, 3 * D_CONV
OFF_XBC = 3 * D_CONV + D_SSM

MIX_TQ = 512
MLP_TM = 1024
MLP_FF_BLK = 1024
COL_BLK = 256
TIE_LEAD = 3
VMEM_LIMIT = 56 * 1024 * 1024


def _rms(v, g):
    ms = jnp.mean(v * v, axis=-1, keepdims=True)
    return v * lax.rsqrt(ms + EPS) * g


def _silu(v):
    hv = 0.5 * v
    return hv * jnp.tanh(hv) + hv


def _split_bf16(v):
    hi = v.astype(BF16)
    lo = (v - hi.astype(F32)).astype(BF16)
    return hi, lo


def _after(v, dep):
    bits = lax.bitcast_convert_type(dep[0:v.shape[0], 0:v.shape[1]], jnp.uint32)
    sixteen = jnp.uint32(16)
    zero = lax.shift_right_logical(lax.shift_right_logical(bits, sixteen), sixteen)
    return lax.bitcast_convert_type(lax.bitcast_convert_type(v, jnp.uint32) | zero, F32)


def _row_groups(v):
    return [v[SUBLANES * i:SUBLANES * (i + 1)] for i in range(v.shape[0] // SUBLANES)]


def _causal_conv(v, prev, taps, bias=None):
    k_taps = len(taps)
    groups = [prev] + _row_groups(v)
    sub = lax.broadcasted_iota(jnp.int32, prev.shape, 0)
    out = [taps[k_taps - 1] * g for g in groups[1:]]
    if bias is not None:
        out = [o + bias for o in out]
    for k in range(k_taps - 1):
        s = k_taps - 1 - k
        rolled = [pltpu.roll(g, s, 0) for g in groups]
        for i in range(len(out)):
            out[i] = out[i] + taps[k] * jnp.where(sub < s, rolled[i], rolled[i + 1])
    return out


def _mixer_kernel(steps_per_seq, x_ref, gpre_ref, wmain_ref, wdt_ref, cwa_ref, cws_ref, cbs_ref,
                  dtb_ref, alog_ref, dsk_ref, gconv_ref, gssm_ref, wout_ref, gpost_ref,
                  expand_ref, o_ref, utail, xtail, yabuf, act, gate, state, ycat):
    tq = x_ref.shape[0]

    @pl.when(pl.program_id(0) % steps_per_seq == 0)
    def _():
        utail[...] = jnp.zeros(utail.shape, F32)
        xtail[...] = jnp.zeros(xtail.shape, F32)
        state[...] = jnp.zeros(state.shape, F32)

    x = x_ref[...]
    h = _rms(x, gpre_ref[...]).astype(BF16)

    def proj(c0, width=COL_BLK):
        return jnp.dot(h, wmain_ref[:, c0:c0 + width], preferred_element_type=F32)

    for c0 in range(0, XBC_DIM, COL_BLK):
        blk = slice(c0, c0 + COL_BLK)
        raw = proj(OFF_XBC + c0)
        taps = [cws_ref[k:k + 1, blk] for k in range(SSM_CONV_K)]
        conv = _causal_conv(raw, xtail[:, blk], taps, cbs_ref[:, blk])
        xtail[:, blk] = raw[tq - SUBLANES:tq]
        act[:, blk] = jnp.concatenate([_silu(g) for g in conv], axis=0)

    dt_raw = jnp.dot(h, wdt_ref[...], preferred_element_type=F32)
    dt = jax.nn.softplus(dt_raw + dtb_ref[...])

    proj_a = {}

    def group_a_block(key, col):
        proj_a[key] = proj(col)
        return proj_a[key]

    deferred = [functools.partial(group_a_block, (kind, c0), off + c0)
                for c0 in range(0, D_CONV, COL_BLK)
                for kind, off in (("xa", OFF_XA), ("ca", OFF_CA), ("ba", OFF_BA))]
    issued = []
    for c0 in range(0, D_SSM, COL_BLK):
        gate[:, c0:c0 + COL_BLK] = _silu(proj(OFF_Z + c0))

    a_row = -jnp.exp(alog_ref[...])
    r_i = lax.broadcasted_iota(jnp.int32, (CHUNK, CHUNK), 0)
    c_i = lax.broadcasted_iota(jnp.int32, (CHUNK, CHUNK), 1)
    causal = r_i >= c_i
    tri = causal.astype(BF16)
    tri2 = jnp.concatenate([tri, tri], axis=1)
    first_head_lanes = c_i < HEAD_DIM
    c_off = D_SSM + N_GROUPS * D_STATE

    for c in range(tq // CHUNK):
        rs = slice(c * CHUNK, (c + 1) * CHUNK)
        dt_c = dt[rs]
        adt = dt_c * a_row
        adt_hi, adt_lo = _split_bf16(adt)
        cs = jnp.dot(tri2, jnp.concatenate([adt_hi, adt_lo], axis=0),
                     preferred_element_type=F32)
        cs_last = cs[CHUNK - 1:CHUNK, :]
        stack = jnp.concatenate(
            [dt_c, jnp.exp(cs_last - cs),
             jnp.broadcast_to(jnp.exp(cs_last), (SUBLANES, DT_PAD))], axis=0)
        s_hi, s_lo = _split_bf16(stack)
        wide = jnp.dot(jnp.concatenate([s_hi, s_lo], axis=1), expand_ref[...],
                       preferred_element_type=F32)
        dt_w = wide[0:CHUNK]
        to_end_w = wide[CHUNK:2 * CHUNK]
        chunk_decay_w = wide[2 * CHUNK:2 * CHUNK + 1]

        xs_c = act[rs, 0:D_SSM]
        xdt = xs_c * dt_w
        cs2 = cs * LOG2E
        cs2_t = cs2.T
        ecs = jnp.exp(cs)
        y_blocks = []
        for g in range(N_GROUPS):
            b_g = act[rs, D_SSM + g * D_STATE:D_SSM + (g + 1) * D_STATE]
            c_g = act[rs, c_off + g * D_STATE:c_off + (g + 1) * D_STATE]
            c_bf = c_g.astype(BF16)
            scores = lax.dot_general(c_bf, b_g.astype(BF16),
                                     (((1,), (1,)), ((), ())), preferred_element_type=F32)
            scores_bf = scores.astype(BF16)
            s_prev = state[g]
            lane0 = g * GROUP_W
            for j in range(HEADS_PER_GROUP // 2):
                xd = xdt[:, lane0 + j * LANES:lane0 + (j + 1) * LANES]
                if deferred:
                    issued.append(deferred.pop(0)())
                if len(issued) > TIE_LEAD:
                    xd = jnp.concatenate(
                        [_after(xd[0:SUBLANES], issued.pop(0)), xd[SUBLANES:]], axis=0)
                rhs = jnp.concatenate(
                    [xd, s_prev[:, j * LANES:(j + 1) * LANES]], axis=0).astype(BF16)
                ys = []
                for e in range(2):
                    hh = g * HEADS_PER_GROUP + 2 * j + e
                    seg = jnp.where(causal, cs2[:, hh:hh + 1] - cs2_t[hh:hh + 1, :], -jnp.inf)
                    m = scores_bf * jnp.exp2(seg).astype(BF16)
                    c_scaled = c_bf * jnp.broadcast_to(ecs[:, hh:hh + 1], (CHUNK, D_STATE)).astype(BF16)
                    lhs = jnp.concatenate([m, c_scaled], axis=1)
                    ys.append(jnp.dot(lhs, rhs, preferred_element_type=F32))
                y_blocks.append(jnp.where(first_head_lanes, ys[0], ys[1]))
            xdt_end = (xdt[:, lane0:lane0 + GROUP_W]
                       * to_end_w[:, lane0:lane0 + GROUP_W]).astype(BF16)
            contrib = jnp.dot(b_g.T.astype(BF16), xdt_end, preferred_element_type=F32)
            state[g] = s_prev * chunk_decay_w[:, lane0:lane0 + GROUP_W] + contrib
        y_c = jnp.concatenate(y_blocks, axis=1) + dsk_ref[...] * xs_c
        y_c = y_c * gate[rs, :]
        gs = gssm_ref[...]
        normed = [
            _rms(y_c[:, g * GROUP_W:(g + 1) * GROUP_W], gs[:, g * GROUP_W:(g + 1) * GROUP_W])
            for g in range(N_GROUPS)]
        ycat[rs, D_CONV:D_CONV + D_SSM] = jnp.concatenate(normed, axis=1).astype(BF16)

    for block in deferred:
        block()

    ssq = jnp.zeros((tq, 1), F32)
    for c0 in range(0, D_CONV, COL_BLK):
        blk = slice(c0, c0 + COL_BLK)
        u = proj_a["xa", c0] * proj_a["ca", c0]
        taps = [cwa_ref[k:k + 1, blk] for k in range(CONV_K)]
        conv = jnp.concatenate(_causal_conv(u, utail[:, blk], taps), axis=0)
        utail[:, blk] = u[tq - SUBLANES:tq]
        ya = proj_a["ba", c0] * conv
        ssq = ssq + jnp.sum(ya * ya, axis=-1, keepdims=True)
        yabuf[:, blk] = ya
    ya_scale = lax.rsqrt(ssq * (1.0 / D_CONV) + EPS)
    ycat[:, 0:D_CONV] = (yabuf[...] * ya_scale * gconv_ref[...]).astype(BF16)

    mix = (jnp.dot(ycat[:, D_CONV:], wout_ref[D_CONV:, :], preferred_element_type=F32)
           + jnp.dot(ycat[:, 0:D_CONV], wout_ref[0:D_CONV, :], preferred_element_type=F32))
    o_ref[...] = x + _rms(mix, gpost_ref[...])


def _mlp_kernel(x_ref, gpre_ref, wup_ref, wdown_ref, gpost_ref, o_ref):
    x = x_ref[...]
    h = _rms(x, gpre_ref[...]).astype(BF16)
    acc = None
    for f0 in range(0, D_FF, MLP_FF_BLK):
        f = jnp.dot(h, wup_ref[:, f0:f0 + MLP_FF_BLK], preferred_element_type=F32)
        f = jnp.square(jnp.maximum(f, 0.0)).astype(BF16)
        part = jnp.dot(f, wdown_ref[f0:f0 + MLP_FF_BLK, :], preferred_element_type=F32)
        acc = part if acc is None else acc + part
    o_ref[...] = x + _rms(acc, gpost_ref[...])


class _Layer:
    def __init__(self, stacked, index):
        self.array, self.index = stacked, index


def _resident(p):
    if isinstance(p, _Layer):
        nd = p.array.ndim - 1
        return pl.BlockSpec((None,) + p.array.shape[1:], lambda *_: (p.index,) + (0,) * nd,
                            pipeline_mode=pl.Buffered(1))
    nd = p.ndim
    return pl.BlockSpec(p.shape, lambda *_: (0,) * nd, pipeline_mode=pl.Buffered(1))


def _operand(p):
    return p.array if isinstance(p, _Layer) else p


def _mixer_call(x2, seq_len, params):
    m, d = x2.shape
    tq = MIX_TQ
    x_spec = pl.BlockSpec((tq, d), lambda i: (i, 0))
    return pl.pallas_call(
        functools.partial(_mixer_kernel, seq_len // tq),
        grid=(m // tq,),
        in_specs=[x_spec] + [_resident(p) for p in params],
        out_specs=x_spec,
        out_shape=jax.ShapeDtypeStruct(x2.shape, x2.dtype),
        scratch_shapes=[
            pltpu.VMEM((SUBLANES, D_CONV), F32),
            pltpu.VMEM((SUBLANES, XBC_DIM), F32),
            pltpu.VMEM((tq, D_CONV), F32),
            pltpu.VMEM((tq, XBC_DIM), F32),
            pltpu.VMEM((tq, D_SSM), F32),
            pltpu.VMEM((N_GROUPS, D_STATE, GROUP_W), F32),
            pltpu.VMEM((tq, D_CONV + D_SSM), BF16),
        ],
        compiler_params=pltpu.CompilerParams(
            dimension_semantics=("arbitrary",), vmem_limit_bytes=VMEM_LIMIT),
        name="mixer",
    )(x2, *[_operand(p) for p in params])


def _mlp_call(x2, params):
    m, d = x2.shape
    x_spec = pl.BlockSpec((MLP_TM, d), lambda i: (i, 0))
    return pl.pallas_call(
        _mlp_kernel,
        grid=(m // MLP_TM,),
        in_specs=[x_spec] + [_resident(p) for p in params],
        out_specs=x_spec,
        out_shape=jax.ShapeDtypeStruct(x2.shape, x2.dtype),
        compiler_params=pltpu.CompilerParams(
            dimension_semantics=("arbitrary",), vmem_limit_bytes=VMEM_LIMIT),
        name="mlp",
    )(x2, *[_operand(p) for p in params])


def _expand_matrix():
    k = lax.broadcasted_iota(jnp.int32, (2 * DT_PAD, D_SSM), 0) % DT_PAD
    j = lax.broadcasted_iota(jnp.int32, (2 * DT_PAD, D_SSM), 1) // HEAD_DIM
    return (k == j).astype(BF16)


def kernel(x, norm_mix_pre, w_in, conv_a_w, ssm_conv_w, ssm_conv_b, dt_bias, a_log, d_skip,
           conv_out_norm, ssm_out_norm, w_out, norm_mix_post, norm_mlp_pre, w_up, w_down,
           norm_mlp_post):
    bsz, t, d = x.shape
    depth = w_in.shape[0]
    assert d == D_MODEL and t % MIX_TQ == 0 and (bsz * t) % MLP_TM == 0
    assert w_in.shape[2] == MAIN_COLS + N_HEADS
    expand = _expand_matrix()
    x = x.reshape(bsz * t, d)
    row = lambda v: v.reshape(1, -1).astype(F32)
    pad_heads = lambda v: jnp.pad(v.astype(F32), (0, DT_PAD - N_HEADS)).reshape(1, DT_PAD)
    w_main = w_in.astype(BF16)
    w_dt = jnp.pad(w_in[:, :, MAIN_COLS:], ((0, 0), (0, 0), (0, DT_PAD - N_HEADS))).astype(BF16)
    w_out_b, w_up_b, w_down_b = w_out.astype(BF16), w_up.astype(BF16), w_down.astype(BF16)
    for i in range(depth):
        mixer_params = (
            row(norm_mix_pre[i]), _Layer(w_main, i), _Layer(w_dt, i), conv_a_w[i].astype(F32),
            ssm_conv_w[i].astype(F32), row(ssm_conv_b[i]), pad_heads(dt_bias[i]),
            pad_heads(a_log[i]), row(jnp.repeat(d_skip[i], HEAD_DIM)), row(conv_out_norm[i]),
            row(ssm_out_norm[i]), _Layer(w_out_b, i), row(norm_mix_post[i]), expand)
        x = _mixer_call(x, t, mixer_params)
        mlp_params = (row(norm_mlp_pre[i]), _Layer(w_up_b, i), _Layer(w_down_b, i),
                      row(norm_mlp_post[i]))
        x = _mlp_call(x, mlp_params)
    return x.reshape(bsz, t, d)
```

```python
import functools

import jax
import jax.numpy as jnp
from jax import lax
from jax.experimental import pallas as pl
from jax.experimental.pallas import tpu as pltpu

F32 = jnp.float32
BF16 = jnp.bfloat16

D_MODEL = 1024
D_CONV = 1024
CONV_K = 3
D_SSM = 1024
HEAD_DIM = 64
N_HEADS = D_SSM // HEAD_DIM
N_GROUPS = 2
HEADS_PER_GROUP = N_HEADS // N_GROUPS
D_STATE = 128
SSM_CONV_K = 4
CHUNK = 128
XBC_DIM = D_SSM + 2 * N_GROUPS * D_STATE
GROUP_W = HEADS_PER_GROUP * HEAD_DIM
D_FF = 4 * D_MODEL
EPS = 1e-6
LOG2E = 1.4426950408889634

LANES = 128
SUBLANES = 8
DT_PAD = LANES
MAIN_COLS = 3 * D_CONV + D_SSM + XBC_DIM
OFF_XA, OFF_CA, OFF_BA, OFF_Z = 0, D_CONV, 2 * D_CONV, 3 * D_CONV
OFF_XBC = 3 * D_CONV + D_SSM

MIX_TQ = 512
MLP_TM = 1024
MLP_FF_BLK = 1024
COL_BLK = 256
TIE_LEAD = 3
VMEM_LIMIT = 56 * 1024 * 1024


def _rms(v, g):
    ms = jnp.mean(v * v, axis=-1, keepdims=True)
    return v * lax.rsqrt(ms + EPS) * g


def _silu(v):
    hv = 0.5 * v
    return hv * jnp.tanh(hv) + hv


def _split_bf16(v):
    hi = v.astype(BF16)
    lo = (v - hi.astype(F32)).astype(BF16)
    return hi, lo


def _after(v, dep):
    bits = lax.bitcast_convert_type(dep[0:v.shape[0], 0:v.shape[1]], jnp.uint32)
    sixteen = jnp.uint32(16)
    zero = lax.shift_right_logical(lax.shift_right_logical(bits, sixteen), sixteen)
    return lax.bitcast_convert_type(lax.bitcast_convert_type(v, jnp.uint32) | zero, F32)


def _row_groups(v):
    return [v[SUBLANES * i:SUBLANES * (i + 1)] for i in range(v.shape[0] // SUBLANES)]


def _causal_conv(v, prev, taps, bias=None):
    k_taps = len(taps)
    groups = [prev] + _row_groups(v)
    sub = lax.broadcasted_iota(jnp.int32, prev.shape, 0)
    out = [taps[k_taps - 1] * g for g in groups[1:]]
    if bias is not None:
        out = [o + bias for o in out]
    for k in range(k_taps - 1):
        s = k_taps - 1 - k
        rolled = [pltpu.roll(g, s, 0) for g in groups]
        for i in range(len(out)):
            out[i] = out[i] + taps[k] * jnp.where(sub < s, rolled[i], rolled[i + 1])
    return out


def _mixer_kernel(steps_per_seq, x_ref, gpre_ref, wmain_ref, wdt_ref, cwa_ref, cws_ref, cbs_ref,
                  dtb_ref, alog_ref, dsk_ref, gconv_ref, gssm_ref, wout_ref, gpost_ref,
                  expand_ref, o_ref, utail, xtail, yabuf, act, gate, state, ycat):
    tq = x_ref.shape[0]

    @pl.when(pl.program_id(0) % steps_per_seq == 0)
    def _():
        utail[...] = jnp.zeros(utail.shape, F32)
        xtail[...] = jnp.zeros(xtail.shape, F32)
        state[...] = jnp.zeros(state.shape, F32)

    x = x_ref[...]
    h = _rms(x, gpre_ref[...]).astype(BF16)

    def proj(c0, width=COL_BLK):
        return jnp.dot(h, wmain_ref[:, c0:c0 + width], preferred_element_type=F32)

    for c0 in range(0, XBC_DIM, COL_BLK):
        blk = slice(c0, c0 + COL_BLK)
        raw = proj(OFF_XBC + c0)
        taps = [cws_ref[k:k + 1, blk] for k in range(SSM_CONV_K)]
        conv = _causal_conv(raw, xtail[:, blk], taps, cbs_ref[:, blk])
        xtail[:, blk] = raw[tq - SUBLANES:tq]
        act[:, blk] = jnp.concatenate([_silu(g) for g in conv], axis=0)

    dt_raw = jnp.dot(h, wdt_ref[...], preferred_element_type=F32)
    dt = jax.nn.softplus(dt_raw + dtb_ref[...])

    proj_a = {}

    def group_a_block(key, col):
        proj_a[key] = proj(col)
        return proj_a[key]

    deferred = [functools.partial(group_a_block, (kind, c0), off + c0)
                for c0 in range(0, D_CONV, COL_BLK)
                for kind, off in (("xa", OFF_XA), ("ca", OFF_CA), ("ba", OFF_BA))]
    issued = []
    n_deferred = len(deferred)
    n_pairs = (tq // CHUNK) * N_GROUPS * (HEADS_PER_GROUP // 2)
    for c0 in range(0, D_SSM, COL_BLK):
        gate[:, c0:c0 + COL_BLK] = _silu(proj(OFF_Z + c0))

    a_row = -jnp.exp(alog_ref[...])
    r_i = lax.broadcasted_iota(jnp.int32, (CHUNK, CHUNK), 0)
    c_i = lax.broadcasted_iota(jnp.int32, (CHUNK, CHUNK), 1)
    causal = r_i >= c_i
    tri = causal.astype(BF16)
    tri2 = jnp.concatenate([tri, tri], axis=1)
    first_head_lanes = c_i < HEAD_DIM
    c_off = D_SSM + N_GROUPS * D_STATE

    for c in range(tq // CHUNK):
        rs = slice(c * CHUNK, (c + 1) * CHUNK)
        dt_c = dt[rs]
        adt = dt_c * a_row
        adt_hi, adt_lo = _split_bf16(adt)
        cs = jnp.dot(tri2, jnp.concatenate([adt_hi, adt_lo], axis=0),
                     preferred_element_type=F32)
        cs_last = cs[CHUNK - 1:CHUNK, :]
        stack = jnp.concatenate(
            [dt_c, jnp.exp(cs_last - cs),
             jnp.broadcast_to(jnp.exp(cs_last), (SUBLANES, DT_PAD))], axis=0)
        s_hi, s_lo = _split_bf16(stack)
        wide = jnp.dot(jnp.concatenate([s_hi, s_lo], axis=1), expand_ref[...],
                       preferred_element_type=F32)
        dt_w = wide[0:CHUNK]
        to_end_w = wide[CHUNK:2 * CHUNK]
        chunk_decay_w = wide[2 * CHUNK:2 * CHUNK + 1]

        xs_c = act[rs, 0:D_SSM]
        xdt = xs_c * dt_w
        cs2 = cs * LOG2E
        cs2_t = cs2.T
        ecs = jnp.exp(cs)
        y_blocks = []
        for g in range(N_GROUPS):
            b_g = act[rs, D_SSM + g * D_STATE:D_SSM + (g + 1) * D_STATE]
            c_g = act[rs, c_off + g * D_STATE:c_off + (g + 1) * D_STATE]
            c_bf = c_g.astype(BF16)
            scores = lax.dot_general(c_bf, b_g.astype(BF16),
                                     (((1,), (1,)), ((), ())), preferred_element_type=F32)
            scores_bf = scores.astype(BF16)
            s_prev = state[g]
            lane0 = g * GROUP_W
            for j in range(HEADS_PER_GROUP // 2):
                xd = xdt[:, lane0 + j * LANES:lane0 + (j + 1) * LANES]
                pair = (c * N_GROUPS + g) * (HEADS_PER_GROUP // 2) + j
                while deferred and n_deferred - len(deferred) < -(-(pair + 1) * n_deferred
                                                                  // (n_pairs - TIE_LEAD)):
                    issued.append((pair + TIE_LEAD, deferred.pop(0)()))
                while issued and issued[0][0] <= pair:
                    xd = jnp.concatenate(
                        [_after(xd[0:SUBLANES], issued.pop(0)[1]), xd[SUBLANES:]], axis=0)
                rhs = jnp.concatenate(
                    [xd, s_prev[:, j * LANES:(j + 1) * LANES]], axis=0).astype(BF16)
                ys = []
                for e in range(2):
                    hh = g * HEADS_PER_GROUP + 2 * j + e
                    seg = jnp.where(causal, cs2[:, hh:hh + 1] - cs2_t[hh:hh + 1, :], -jnp.inf)
                    m = scores_bf * jnp.exp2(seg).astype(BF16)
                    c_scaled = c_bf * jnp.broadcast_to(ecs[:, hh:hh + 1], (CHUNK, D_STATE)).astype(BF16)
                    lhs = jnp.concatenate([m, c_scaled], axis=1)
                    ys.append(jnp.dot(lhs, rhs, preferred_element_type=F32))
                y_blocks.append(jnp.where(first_head_lanes, ys[0], ys[1]))
            xdt_end = (xdt[:, lane0:lane0 + GROUP_W]
                       * to_end_w[:, lane0:lane0 + GROUP_W]).astype(BF16)
            contrib = jnp.dot(b_g.T.astype(BF16), xdt_end, preferred_element_type=F32)
            state[g] = s_prev * chunk_decay_w[:, lane0:lane0 + GROUP_W] + contrib
        y_c = jnp.concatenate(y_blocks, axis=1) + dsk_ref[...] * xs_c
        y_c = y_c * gate[rs, :]
        gs = gssm_ref[...]
        normed = [
            _rms(y_c[:, g * GROUP_W:(g + 1) * GROUP_W], gs[:, g * GROUP_W:(g + 1) * GROUP_W])
            for g in range(N_GROUPS)]
        ycat[rs, D_CONV:D_CONV + D_SSM] = jnp.concatenate(normed, axis=1).astype(BF16)

    for block in deferred:
        block()

    ssq = jnp.zeros((tq, 1), F32)
    for c0 in range(0, D_CONV, COL_BLK):
        blk = slice(c0, c0 + COL_BLK)
        u = proj_a["xa", c0] * proj_a["ca", c0]
        taps = [cwa_ref[k:k + 1, blk] for k in range(CONV_K)]
        conv = jnp.concatenate(_causal_conv(u, utail[:, blk], taps), axis=0)
        utail[:, blk] = u[tq - SUBLANES:tq]
        ya = proj_a["ba", c0] * conv
        ssq = ssq + jnp.sum(ya * ya, axis=-1, keepdims=True)
        yabuf[:, blk] = ya
    ya_scale = lax.rsqrt(ssq * (1.0 / D_CONV) + EPS)
    ycat[:, 0:D_CONV] = (yabuf[...] * ya_scale * gconv_ref[...]).astype(BF16)

    mix = (jnp.dot(ycat[:, D_CONV:], wout_ref[D_CONV:, :], preferred_element_type=F32)
           + jnp.dot(ycat[:, 0:D_CONV], wout_ref[0:D_CONV, :], preferred_element_type=F32))
    o_ref[...] = x + _rms(mix, gpost_ref[...])


def _mlp_kernel(x_ref, gpre_ref, wup_ref, wdown_ref, gpost_ref, o_ref):
    x = x_ref[...]
    h = _rms(x, gpre_ref[...]).astype(BF16)
    acc = None
    for f0 in range(0, D_FF, MLP_FF_BLK):
        f = jnp.dot(h, wup_ref[:, f0:f0 + MLP_FF_BLK], preferred_element_type=F32)
        f = jnp.square(jnp.maximum(f, 0.0)).astype(BF16)
        part = jnp.dot(f, wdown_ref[f0:f0 + MLP_FF_BLK, :], preferred_element_type=F32)
        acc = part if acc is None else acc + part
    o_ref[...] = x + _rms(acc, gpost_ref[...])


class _Layer:
    def __init__(self, stacked, index):
        self.array, self.index = stacked, index


def _resident(p):
    if isinstance(p, _Layer):
        nd = p.array.ndim - 1
        return pl.BlockSpec((None,) + p.array.shape[1:], lambda *_: (p.index,) + (0,) * nd,
                            pipeline_mode=pl.Buffered(1))
    nd = p.ndim
    return pl.BlockSpec(p.shape, lambda *_: (0,) * nd, pipeline_mode=pl.Buffered(1))


def _operand(p):
    return p.array if isinstance(p, _Layer) else p


def _mixer_call(x2, seq_len, params):
    m, d = x2.shape
    tq = MIX_TQ
    x_spec = pl.BlockSpec((tq, d), lambda i: (i, 0))
    return pl.pallas_call(
        functools.partial(_mixer_kernel, seq_len // tq),
        grid=(m // tq,),
        in_specs=[x_spec] + [_resident(p) for p in params],
        out_specs=x_spec,
        out_shape=jax.ShapeDtypeStruct(x2.shape, x2.dtype),
        scratch_shapes=[
            pltpu.VMEM((SUBLANES, D_CONV), F32),
            pltpu.VMEM((SUBLANES, XBC_DIM), F32),
            pltpu.VMEM((tq, D_CONV), F32),
            pltpu.VMEM((tq, XBC_DIM), F32),
            pltpu.VMEM((tq, D_SSM), F32),
            pltpu.VMEM((N_GROUPS, D_STATE, GROUP_W), F32),
            pltpu.VMEM((tq, D_CONV + D_SSM), BF16),
        ],
        compiler_params=pltpu.CompilerParams(
            dimension_semantics=("arbitrary",), vmem_limit_bytes=VMEM_LIMIT),
        name="mixer",
    )(x2, *[_operand(p) for p in params])


def _mlp_call(x2, params):
    m, d = x2.shape
    x_spec = pl.BlockSpec((MLP_TM, d), lambda i: (i, 0))
    return pl.pallas_call(
        _mlp_kernel,
        grid=(m // MLP_TM,),
        in_specs=[x_spec] + [_resident(p) for p in params],
        out_specs=x_spec,
        out_shape=jax.ShapeDtypeStruct(x2.shape, x2.dtype),
        compiler_params=pltpu.CompilerParams(
            dimension_semantics=("arbitrary",), vmem_limit_bytes=VMEM_LIMIT),
        name="mlp",
    )(x2, *[_operand(p) for p in params])


def _expand_matrix():
    k = lax.broadcasted_iota(jnp.int32, (2 * DT_PAD, D_SSM), 0) % DT_PAD
    j = lax.broadcasted_iota(jnp.int32, (2 * DT_PAD, D_SSM), 1) // HEAD_DIM
    return (k == j).astype(BF16)


def kernel(x, norm_mix_pre, w_in, conv_a_w, ssm_conv_w, ssm_conv_b, dt_bias, a_log, d_skip,
           conv_out_norm, ssm_out_norm, w_out, norm_mix_post, norm_mlp_pre, w_up, w_down,
           norm_mlp_post):
    bsz, t, d = x.shape
    depth = w_in.shape[0]
    assert d == D_MODEL and t % MIX_TQ == 0 and (bsz * t) % MLP_TM == 0
    assert w_in.shape[2] == MAIN_COLS + N_HEADS
    expand = _expand_matrix()
    x = x.reshape(bsz * t, d)
    row = lambda v: v.reshape(1, -1).astype(F32)
    pad_heads = lambda v: jnp.pad(v.astype(F32), (0, DT_PAD - N_HEADS)).reshape(1, DT_PAD)
    w_main = w_in.astype(BF16)
    w_dt = jnp.pad(w_in[:, :, MAIN_COLS:], ((0, 0), (0, 0), (0, DT_PAD - N_HEADS))).astype(BF16)
    w_out_b, w_up_b, w_down_b = w_out.astype(BF16), w_up.astype(BF16), w_down.astype(BF16)
    for i in range(depth):
        mixer_params = (
            row(norm_mix_pre[i]), _Layer(w_main, i), _Layer(w_dt, i), conv_a_w[i].astype(F32),
            ssm_conv_w[i].astype(F32), row(ssm_conv_b[i]), pad_heads(dt_bias[i]),
            pad_heads(a_log[i]), row(jnp.repeat(d_skip[i], HEAD_DIM)), row(conv_out_norm[i]),
            row(ssm_out_norm[i]), _Layer(w_out_b, i), row(norm_mix_post[i]), expand)
        x = _mixer_call(x, t, mixer_params)
        mlp_params = (row(norm_mlp_pre[i]), _Layer(w_up_b, i), _Layer(w_down_b, i),
                      row(norm_mlp_post[i]))
        x = _mlp_call(x, mlp_params)
    return x.reshape(bsz, t, d)
```

```python
import functools

import jax
import jax.numpy as jnp
from jax import lax
from jax.experimental import pallas as pl
from jax.experimental.pallas import tpu as pltpu

F32 = jnp.float32
BF16 = jnp.bfloat16

D_MODEL = 1024
D_CONV = 1024
CONV_K = 3
D_SSM = 1024
HEAD_DIM = 64
N_HEADS = D_SSM // HEAD_DIM
N_GROUPS = 2
HEADS_PER_GROUP = N_HEADS // N_GROUPS
D_STATE = 128
SSM_CONV_K = 4
CHUNK = 128
XBC_DIM = D_SSM + 2 * N_GROUPS * D_STATE
GROUP_W = HEADS_PER_GROUP * HEAD_DIM
D_FF = 4 * D_MODEL
EPS = 1e-6
LOG2E = 1.4426950408889634

LANES = 128
SUBLANES = 8
DT_PAD = LANES
MAIN_COLS = 3 * D_CONV + D_SSM + XBC_DIM
OFF_XA, OFF_CA, OFF_BA, OFF_Z = 0, D_CONV, 2 * D_CONV, 3 * D_CONV
OFF_XBC = 3 * D_CONV + D_SSM

MIX_TQ = 512
MLP_TM = 1024
MLP_FF_BLK = 1024
COL_BLK = 256
TIE_LEAD = 3
VMEM_LIMIT = 56 * 1024 * 1024


def _rms(v, g):
    ms = jnp.mean(v * v, axis=-1, keepdims=True)
    return v * lax.rsqrt(ms + EPS) * g


def _silu(v):
    hv = 0.5 * v
    return hv * jnp.tanh(hv) + hv


def _split_bf16(v):
    hi = v.astype(BF16)
    lo = (v - hi.astype(F32)).astype(BF16)
    return hi, lo


def _after(v, dep):
    bits = lax.bitcast_convert_type(dep[0:v.shape[0], 0:v.shape[1]], jnp.uint32)
    sixteen = jnp.uint32(16)
    zero = lax.shift_right_logical(lax.shift_right_logical(bits, sixteen), sixteen)
    return lax.bitcast_convert_type(lax.bitcast_convert_type(v, jnp.uint32) | zero, F32)


def _row_groups(v):
    return [v[SUBLANES * i:SUBLANES * (i + 1)] for i in range(v.shape[0] // SUBLANES)]


def _causal_conv(v, prev, taps, bias=None):
    k_taps = len(taps)
    groups = [prev] + _row_groups(v)
    sub = lax.broadcasted_iota(jnp.int32, prev.shape, 0)
    out = [taps[k_taps - 1] * g for g in groups[1:]]
    if bias is not None:
        out = [o + bias for o in out]
    for k in range(k_taps - 1):
        s = k_taps - 1 - k
        rolled = [pltpu.roll(g, s, 0) for g in groups]
        for i in range(len(out)):
            out[i] = out[i] + taps[k] * jnp.where(sub < s, rolled[i], rolled[i + 1])
    return out


def _mixer_kernel(steps_per_seq, x_ref, gpre_ref, wmain_ref, wdt_ref, cwa_ref, cws_ref, cbs_ref,
                  dtb_ref, alog_ref, dsk_ref, gconv_ref, gssm_ref, wout_ref, gpost_ref,
                  expand_ref, o_ref, utail, xtail, yabuf, act, gate, state, ycat):
    tq = x_ref.shape[0]

    @pl.when(pl.program_id(0) % steps_per_seq == 0)
    def _():
        utail[...] = jnp.zeros(utail.shape, F32)
        xtail[...] = jnp.zeros(xtail.shape, F32)
        state[...] = jnp.zeros(state.shape, F32)

    x = x_ref[...]
    h = _rms(x, gpre_ref[...]).astype(BF16)

    def proj(c0, width=COL_BLK):
        return jnp.dot(h, wmain_ref[:, c0:c0 + width], preferred_element_type=F32)

    for c0 in range(0, XBC_DIM, COL_BLK):
        blk = slice(c0, c0 + COL_BLK)
        raw = proj(OFF_XBC + c0)
        taps = [cws_ref[k:k + 1, blk] for k in range(SSM_CONV_K)]
        conv = _causal_conv(raw, xtail[:, blk], taps, cbs_ref[:, blk])
        xtail[:, blk] = raw[tq - SUBLANES:tq]
        act[:, blk] = jnp.concatenate([_silu(g) for g in conv], axis=0)

    dt_raw = jnp.dot(h, wdt_ref[...], preferred_element_type=F32)
    dt = jax.nn.softplus(dt_raw + dtb_ref[...])

    proj_a = {}

    def group_a_block(key, col):
        proj_a[key] = proj(col)
        return proj_a[key]

    deferred = [functools.partial(group_a_block, (kind, c0), off + c0)
                for c0 in range(0, D_CONV, COL_BLK)
                for kind, off in (("xa", OFF_XA), ("ca", OFF_CA), ("ba", OFF_BA))]
    issued = []
    n_deferred = len(deferred)
    n_pairs = (tq // CHUNK) * N_GROUPS * (HEADS_PER_GROUP // 2)
    for c0 in range(0, D_SSM, COL_BLK):
        gate[:, c0:c0 + COL_BLK] = _silu(proj(OFF_Z + c0))

    a_row = -jnp.exp(alog_ref[...])
    r_i = lax.broadcasted_iota(jnp.int32, (CHUNK, CHUNK), 0)
    c_i = lax.broadcasted_iota(jnp.int32, (CHUNK, CHUNK), 1)
    causal = r_i >= c_i
    tri = causal.astype(BF16)
    tri2 = jnp.concatenate([tri, tri], axis=1)
    first_head_lanes = c_i < HEAD_DIM
    c_off = D_SSM + N_GROUPS * D_STATE

    for c in range(tq // CHUNK):
        rs = slice(c * CHUNK, (c + 1) * CHUNK)
        dt_c = dt[rs]
        adt = dt_c * a_row
        adt_hi, adt_lo = _split_bf16(adt)
        cs = jnp.dot(tri2, jnp.concatenate([adt_hi, adt_lo], axis=0),
                     preferred_element_type=F32)
        cs_last = cs[CHUNK - 1:CHUNK, :]
        stack = jnp.concatenate(
            [dt_c, jnp.exp(cs_last - cs),
             jnp.broadcast_to(jnp.exp(cs_last), (SUBLANES, DT_PAD))], axis=0)
        s_hi, s_lo = _split_bf16(stack)
        wide = jnp.dot(jnp.concatenate([s_hi, s_lo], axis=1), expand_ref[...],
                       preferred_element_type=F32)
        dt_w = wide[0:CHUNK]
        to_end_w = wide[CHUNK:2 * CHUNK]
        chunk_decay_w = wide[2 * CHUNK:2 * CHUNK + 1]

        xs_c = act[rs, 0:D_SSM]
        xdt = xs_c * dt_w
        cs2 = cs * LOG2E
        cs2_t = cs2.T
        ecs = jnp.exp(cs)
        y_blocks = []
        for g in range(N_GROUPS):
            b_g = act[rs, D_SSM + g * D_STATE:D_SSM + (g + 1) * D_STATE]
            c_g = act[rs, c_off + g * D_STATE:c_off + (g + 1) * D_STATE]
            c_bf = c_g.astype(BF16)
            scores = lax.dot_general(c_bf, b_g.astype(BF16),
                                     (((1,), (1,)), ((), ())), preferred_element_type=F32)
            scores_bf = scores.astype(BF16)
            s_prev = state[g]
            lane0 = g * GROUP_W
            for j in range(HEADS_PER_GROUP // 2):
                xd = xdt[:, lane0 + j * LANES:lane0 + (j + 1) * LANES]
                pair = (c * N_GROUPS + g) * (HEADS_PER_GROUP // 2) + j
                while deferred and n_deferred - len(deferred) < -(-(pair + 1) * n_deferred
                                                                  // (n_pairs - TIE_LEAD)):
                    issued.append((pair + TIE_LEAD, deferred.pop(0)()))
                while issued and issued[0][0] <= pair:
                    xd = jnp.concatenate(
                        [_after(xd[0:SUBLANES], issued.pop(0)[1]), xd[SUBLANES:]], axis=0)
                rhs = jnp.concatenate(
                    [xd, s_prev[:, j * LANES:(j + 1) * LANES]], axis=0).astype(BF16)
                ys = []
                for e in range(2):
                    hh = g * HEADS_PER_GROUP + 2 * j + e
                    seg = jnp.where(causal, cs2[:, hh:hh + 1] - cs2_t[hh:hh + 1, :], -jnp.inf)
                    m = scores_bf * jnp.exp2(seg).astype(BF16)
                    c_scaled = c_bf * jnp.broadcast_to(ecs[:, hh:hh + 1], (CHUNK, D_STATE)).astype(BF16)
                    lhs = jnp.concatenate([m, c_scaled], axis=1)
                    ys.append(jnp.dot(lhs, rhs, preferred_element_type=F32))
                y_blocks.append(jnp.where(first_head_lanes, ys[0], ys[1]))
            xdt_end = (xdt[:, lane0:lane0 + GROUP_W]
                       * to_end_w[:, lane0:lane0 + GROUP_W]).astype(BF16)
            contrib = jnp.dot(b_g.T.astype(BF16), xdt_end, preferred_element_type=F32)
            state[g] = s_prev * chunk_decay_w[:, lane0:lane0 + GROUP_W] + contrib
        y_c = jnp.concatenate(y_blocks, axis=1) + dsk_ref[...] * xs_c
        y_c = y_c * gate[rs, :]
        gs = gssm_ref[...]
        normed = [
            _rms(y_c[:, g * GROUP_W:(g + 1) * GROUP_W], gs[:, g * GROUP_W:(g + 1) * GROUP_W])
            for g in range(N_GROUPS)]
        ycat[rs, D_CONV:D_CONV + D_SSM] = jnp.concatenate(normed, axis=1).astype(BF16)

    for block in deferred:
        block()

    ssq = jnp.zeros((tq, 1), F32)
    for c0 in range(0, D_CONV, COL_BLK):
        blk = slice(c0, c0 + COL_BLK)
        u = proj_a["xa", c0] * proj_a["ca", c0]
        taps = [cwa_ref[k:k + 1, blk] for k in range(CONV_K)]
        conv = jnp.concatenate(_causal_conv(u, utail[:, blk], taps), axis=0)
        utail[:, blk] = u[tq - SUBLANES:tq]
        ya = proj_a["ba", c0] * conv
        ssq = ssq + jnp.sum(ya * ya, axis=-1, keepdims=True)
        yabuf[:, blk] = ya
    ya_scale = lax.rsqrt(ssq * (1.0 / D_CONV) + EPS)
    ycat[:, 0:D_CONV] = (yabuf[...] * ya_scale * gconv_ref[...]).astype(BF16)

    mix = (jnp.dot(ycat[:, D_CONV:], wout_ref[D_CONV:, :], preferred_element_type=F32)
           + jnp.dot(ycat[:, 0:D_CONV], wout_ref[0:D_CONV, :], preferred_element_type=F32))
    o_ref[...] = x + _rms(mix, gpost_ref[...])


def _mlp_kernel(n_cast, x_ref, gpre_ref, wup_ref, wdown_ref, gpost_ref, *rest):
    o_ref = rest[n_cast]
    for src, dst in zip(rest[:n_cast], rest[n_cast + 1:]):
        dst[...] = src[...].astype(BF16)
    x = x_ref[...]
    h = _rms(x, gpre_ref[...]).astype(BF16)
    acc = None
    for f0 in range(0, D_FF, MLP_FF_BLK):
        f = jnp.dot(h, wup_ref[:, f0:f0 + MLP_FF_BLK], preferred_element_type=F32)
        f = jnp.square(jnp.maximum(f, 0.0)).astype(BF16)
        part = jnp.dot(f, wdown_ref[f0:f0 + MLP_FF_BLK, :], preferred_element_type=F32)
        acc = part if acc is None else acc + part
    o_ref[...] = x + _rms(acc, gpost_ref[...])


class _Layer:
    def __init__(self, stacked, index):
        self.array, self.index = stacked, index


def _resident(p):
    if isinstance(p, _Layer):
        nd = p.array.ndim - 1
        return pl.BlockSpec((None,) + p.array.shape[1:], lambda *_: (p.index,) + (0,) * nd,
                            pipeline_mode=pl.Buffered(1))
    nd = p.ndim
    return pl.BlockSpec(p.shape, lambda *_: (0,) * nd, pipeline_mode=pl.Buffered(1))


def _operand(p):
    return p.array if isinstance(p, _Layer) else p


def _mixer_call(x2, seq_len, params):
    m, d = x2.shape
    tq = MIX_TQ
    x_spec = pl.BlockSpec((tq, d), lambda i: (i, 0))
    return pl.pallas_call(
        functools.partial(_mixer_kernel, seq_len // tq),
        grid=(m // tq,),
        in_specs=[x_spec] + [_resident(p) for p in params],
        out_specs=x_spec,
        out_shape=jax.ShapeDtypeStruct(x2.shape, x2.dtype),
        scratch_shapes=[
            pltpu.VMEM((SUBLANES, D_CONV), F32),
            pltpu.VMEM((SUBLANES, XBC_DIM), F32),
            pltpu.VMEM((tq, D_CONV), F32),
            pltpu.VMEM((tq, XBC_DIM), F32),
            pltpu.VMEM((tq, D_SSM), F32),
            pltpu.VMEM((N_GROUPS, D_STATE, GROUP_W), F32),
            pltpu.VMEM((tq, D_CONV + D_SSM), BF16),
        ],
        compiler_params=pltpu.CompilerParams(
            dimension_semantics=("arbitrary",), vmem_limit_bytes=VMEM_LIMIT),
        name="mixer",
    )(x2, *[_operand(p) for p in params])


def _mlp_call(x2, params, cast_next=()):
    m, d = x2.shape
    steps = m // MLP_TM
    x_spec = pl.BlockSpec((MLP_TM, d), lambda i: (i, 0))
    cast_in, cast_out, cast_shapes = [], [], []
    for stacked, layer in cast_next:
        _, rows, cols = stacked.shape
        cast_in.append(pl.BlockSpec((None, rows // steps, cols),
                                    lambda i, layer=layer: (layer, i, 0)))
        cast_out.append(pl.BlockSpec((rows // steps, cols), lambda i: (i, 0)))
        cast_shapes.append(jax.ShapeDtypeStruct((rows, cols), BF16))
    outs = pl.pallas_call(
        functools.partial(_mlp_kernel, len(cast_next)),
        grid=(steps,),
        in_specs=[x_spec] + [_resident(p) for p in params] + cast_in,
        out_specs=[x_spec] + cast_out,
        out_shape=[jax.ShapeDtypeStruct(x2.shape, x2.dtype)] + cast_shapes,
        compiler_params=pltpu.CompilerParams(
            dimension_semantics=("arbitrary",), vmem_limit_bytes=VMEM_LIMIT),
        name="mlp",
    )(x2, *[_operand(p) for p in params], *[stacked for stacked, _ in cast_next])
    return outs[0], list(outs[1:])


def _expand_matrix():
    k = lax.broadcasted_iota(jnp.int32, (2 * DT_PAD, D_SSM), 0) % DT_PAD
    j = lax.broadcasted_iota(jnp.int32, (2 * DT_PAD, D_SSM), 1) // HEAD_DIM
    return (k == j).astype(BF16)


def kernel(x, norm_mix_pre, w_in, conv_a_w, ssm_conv_w, ssm_conv_b, dt_bias, a_log, d_skip,
           conv_out_norm, ssm_out_norm, w_out, norm_mix_post, norm_mlp_pre, w_up, w_down,
           norm_mlp_post):
    bsz, t, d = x.shape
    depth = w_in.shape[0]
    assert d == D_MODEL and t % MIX_TQ == 0 and (bsz * t) % MLP_TM == 0
    assert w_in.shape[2] == MAIN_COLS + N_HEADS
    expand = _expand_matrix()
    x = x.reshape(bsz * t, d)
    row = lambda v: v.reshape(1, -1).astype(F32)
    pad_heads = lambda v: jnp.pad(v.astype(F32), (0, DT_PAD - N_HEADS)).reshape(1, DT_PAD)
    w_dt = jnp.pad(w_in[:, :, MAIN_COLS:], ((0, 0), (0, 0), (0, DT_PAD - N_HEADS))).astype(BF16)
    stacked = (w_in, w_out, w_up, w_down)
    w_main, w_out_b, w_up_b, w_down_b = (w[0].astype(BF16) for w in stacked)
    for i in range(depth):
        mixer_params = (
            row(norm_mix_pre[i]), w_main, _Layer(w_dt, i), conv_a_w[i].astype(F32),
            ssm_conv_w[i].astype(F32), row(ssm_conv_b[i]), pad_heads(dt_bias[i]),
            pad_heads(a_log[i]), row(jnp.repeat(d_skip[i], HEAD_DIM)), row(conv_out_norm[i]),
            row(ssm_out_norm[i]), w_out_b, row(norm_mix_post[i]), expand)
        x = _mixer_call(x, t, mixer_params)
        mlp_params = (row(norm_mlp_pre[i]), w_up_b, w_down_b, row(norm_mlp_post[i]))
        cast_next = [(w, i + 1) for w in stacked] if i + 1 < depth else []
        x, casts = _mlp_call(x, mlp_params, cast_next)
        if casts:
            w_main, w_out_b, w_up_b, w_down_b = casts
    return x.reshape(bsz, t, d)
```
